```python
import jax, jax.numpy as jnp
from jax import lax
import numpy as np

D_MODEL = 1024
BATCH = 8
SEQ = 4096
DEPTH = 4

CHUNK = 64
N_MIXERS = 3
N_POOL_LAYERS = (DEPTH + 2) // 3
N_CONV_LAYERS = (DEPTH + 1) // 3
N_GMLP_LAYERS = DEPTH // 3
EPS = 1e-6

POOL_WINDOWS = (2, 4, 8, 16)
POOL_GROUP = D_MODEL // len(POOL_WINDOWS)

CONV_WIDTH = 3

GMLP_BLOCK = 128
GMLP_HALF = 3 * D_MODEL
GMLP_GROUPS = 8
GMLP_GROUP_DIM = GMLP_HALF // GMLP_GROUPS

PEER_HEADS = 8
PEER_NKEYS = 128
PEER_EXPERTS = PEER_NKEYS * PEER_NKEYS
PEER_DKEY = 256
PEER_DHALF = PEER_DKEY // 2
PEER_TOPK = 16
PEER_TOKEN_BLOCK = 128

kernel_name = "hybrid_pool_conv_gmlp_peer_trunk"


def rmsnorm(x, g):
    xf = x.astype(jnp.float32)
    y = xf * lax.rsqrt(jnp.mean(xf * xf, axis=-1, keepdims=True) + EPS)
    return (y * g.astype(jnp.float32)).astype(x.dtype)


def layernorm(x, g, b):
    xf = x.astype(jnp.float32)
    mu = jnp.mean(xf, axis=-1, keepdims=True)
    var = jnp.mean(jnp.square(xf - mu), axis=-1, keepdims=True)
    y = (xf - mu) * lax.rsqrt(var + EPS)
    return (y * g.astype(jnp.float32) + b.astype(jnp.float32)).astype(x.dtype)


def modulate(h, shift, scale):
    return h * (1 + scale[:, None, :]) + shift[:, None, :]


def pool_mixer(h, w_in, w_grp, scale, w_out):
    s = h.shape[1]
    u = h @ w_in
    cs = jnp.cumsum(u.astype(jnp.float32), axis=1)
    t = jnp.arange(1, s + 1, dtype=jnp.float32)[None, :, None]
    outs = []
    for g, w in enumerate(POOL_WINDOWS):
        lo, hi = g * POOL_GROUP, (g + 1) * POOL_GROUP
        cg = cs[..., lo:hi]
        lag = jnp.pad(cg, ((0, 0), (w, 0), (0, 0)))[:, :s]
        mean = (cg - lag) / jnp.minimum(t, float(w))
        p = mean.astype(u.dtype) - u[..., lo:hi]
        outs.append(p @ w_grp[g])
    z = jnp.concatenate(outs, axis=-1) * scale
    return z @ w_out


def conv_mixer(h, w_in, conv_w, w_out):
    bg, cg, xt = jnp.split(h @ w_in, 3, axis=-1)
    z = cg * xt
    zc = lax.conv_general_dilated(
        z, conv_w[:, None, :].astype(z.dtype),
        window_strides=(1,), padding=[(CONV_WIDTH - 1, 0)],
        dimension_numbers=('NWC', 'WIO', 'NWC'),
        feature_group_count=D_MODEL)
    return (bg * zc) @ w_out


def gmlp_mixer(h, w_in, ln_g, ln_b, w_s, b_s, w_out):
    b, s, _ = h.shape
    z = jax.nn.gelu(h @ w_in)
    u, v = jnp.split(z, 2, axis=-1)
    v = layernorm(v, ln_g, ln_b)
    v = v.reshape(b, s // GMLP_BLOCK, GMLP_BLOCK, GMLP_GROUPS, GMLP_GROUP_DIM)
    pos = jnp.arange(GMLP_BLOCK) // CHUNK
    mask = pos[None, :] <= pos[:, None]
    ws = jnp.where(mask[None], w_s, jnp.zeros_like(w_s))
    vs = jnp.einsum('gij,bnjgc->bnigc', ws, v) + b_s.T[None, None, :, :, None]
    return (u * vs.reshape(b, s, GMLP_HALF)) @ w_out


def peer_ffn(h, w_q, keys, u_tab, v_tab):
    b, s, d = h.shape

    def one_block(hb):
        t = hb.shape[0]
        q = (hb @ w_q).reshape(t, PEER_HEADS, 2, PEER_DHALF)
        sub = jnp.einsum('thpd,hpkd->thpk', q, keys)
        sv, si = lax.top_k(sub, PEER_TOPK)
        cand_s = (sv[:, :, 0, :, None] + sv[:, :, 1, None, :]).reshape(t, PEER_HEADS, PEER_TOPK * PEER_TOPK)
        cand_e = (si[:, :, 0, :, None] * PEER_NKEYS + si[:, :, 1, None, :]).reshape(t, PEER_HEADS, PEER_TOPK * PEER_TOPK)
        top_s, top_pos = lax.top_k(cand_s, PEER_TOPK)
        expert = jnp.take_along_axis(cand_e, top_pos, axis=-1)
        g = jax.nn.softmax(top_s.astype(jnp.float32), axis=-1).astype(hb.dtype)
        act = jax.nn.gelu(jnp.einsum('thkd,td->thk', u_tab[expert], hb))
        return jnp.einsum('thk,thkd->td', g * act, v_tab[expert])

    out = lax.map(one_block, h.reshape(-1, PEER_TOKEN_BLOCK, d))
    return out.reshape(b, s, d)


def setup_inputs(seed: int = 0) -> dict:
    key = jax.random.key(seed)
    ks = iter(jax.random.split(key, 32))
    D = D_MODEL

    def nrm(shape, scale):
        return jax.random.normal(next(ks), shape, jnp.float32) * scale

    return {
        "x": nrm((BATCH, SEQ, D), 1.0),
        "c": nrm((BATCH, D), 1.0),
        "ada_w": nrm((DEPTH, D, 6 * D), 0.5 * D ** -0.5),
        "ada_b": nrm((DEPTH, 6 * D), 0.02),
        "norm_g": 1.0 + nrm((DEPTH, 2, D), 0.05),
        "pool_w_in": nrm((N_POOL_LAYERS, D, D), D ** -0.5),
        "pool_w_grp": nrm((N_POOL_LAYERS, len(POOL_WINDOWS), POOL_GROUP, POOL_GROUP), POOL_GROUP ** -0.5),
        "pool_scale": 1.0 + nrm((N_POOL_LAYERS, D), 0.1),
        "pool_w_out": nrm((N_POOL_LAYERS, D, D), D ** -0.5),
        "conv_w_in": nrm((N_CONV_LAYERS, D, 3 * D), D ** -0.5),
        "conv_w": nrm((N_CONV_LAYERS, CONV_WIDTH, D), CONV_WIDTH ** -0.5),
        "conv_w_out": nrm((N_CONV_LAYERS, D, D), D ** -0.5),
        "gmlp_w_in": nrm((N_GMLP_LAYERS, D, 2 * GMLP_HALF), D ** -0.5),
        "gmlp_ln_g": 1.0 + nrm((N_GMLP_LAYERS, GMLP_HALF), 0.05),
        "gmlp_ln_b": nrm((N_GMLP_LAYERS, GMLP_HALF), 0.02),
        "gmlp_w_s": nrm((N_GMLP_LAYERS, GMLP_GROUPS, GMLP_BLOCK, GMLP_BLOCK), GMLP_BLOCK ** -0.5),
        "gmlp_b_s": 1.0 + nrm((N_GMLP_LAYERS, GMLP_GROUPS, GMLP_BLOCK), 0.1),
        "gmlp_w_out": nrm((N_GMLP_LAYERS, GMLP_HALF, D), GMLP_HALF ** -0.5),
        "peer_w_q": nrm((DEPTH, D, PEER_HEADS * PEER_DKEY), D ** -0.5),
        "peer_keys": nrm((DEPTH, PEER_HEADS, 2, PEER_NKEYS, PEER_DHALF), PEER_DHALF ** -0.5),
        "peer_u": nrm((DEPTH, PEER_EXPERTS, D), D ** -0.5),
        "peer_v": nrm((DEPTH, PEER_EXPERTS, D), 0.5),
        "final_g": 1.0 + nrm((D,), 0.05),
    }


def reference(x, c, ada_w, ada_b, norm_g,
              pool_w_in, pool_w_grp, pool_scale, pool_w_out,
              conv_w_in, conv_w, conv_w_out,
              gmlp_w_in, gmlp_ln_g, gmlp_ln_b, gmlp_w_s, gmlp_b_s, gmlp_w_out,
              peer_w_q, peer_keys, peer_u, peer_v, final_g):
    cond = jax.nn.silu(c)
    for i in range(DEPTH):
        mod = cond @ ada_w[i] + ada_b[i]
        sh1, sc1, g1, sh2, sc2, g2 = jnp.split(mod, 6, axis=-1)
        h = modulate(rmsnorm(x, norm_g[i, 0]), sh1, sc1)
        kind, j = i % N_MIXERS, i // N_MIXERS
        if kind == 0:
            y = pool_mixer(h, pool_w_in[j], pool_w_grp[j], pool_scale[j], pool_w_out[j])
        elif kind == 1:
            y = conv_mixer(h, conv_w_in[j], conv_w[j], conv_w_out[j])
        else:
            y = gmlp_mixer(h, gmlp_w_in[j], gmlp_ln_g[j], gmlp_ln_b[j],
                           gmlp_w_s[j], gmlp_b_s[j], gmlp_w_out[j])
        x = x + g1[:, None, :] * y
        h = modulate(rmsnorm(x, norm_g[i, 1]), sh2, sc2)
        x = x + g2[:, None, :] * peer_ffn(h, peer_w_q[i], peer_keys[i], peer_u[i], peer_v[i])
    return rmsnorm(x, final_g)
```

```python
import functools

import jax
import jax.numpy as jnp
from jax import lax
from jax.experimental import pallas as pl
from jax.experimental.pallas import tpu as pltpu

EPS = 1e-6
CHUNK = 64
POOL_WINDOWS = (2, 4, 8, 16)
POOL_HALO = 16
CONV_WIDTH = 3
CONV_HALO = 8
GMLP_BLOCK = 128
GMLP_GROUPS = 8
PEER_HEADS = 8
PEER_TOPK = 16
N_MIXERS = 3

LANES = 128
VMEM_LIMIT_BYTES = 56 * 1024 * 1024

BF16 = jnp.bfloat16
F32 = jnp.float32


def _rms_modulate(x, g, shift, scale):
    r = lax.rsqrt(jnp.mean(x * x, axis=-1, keepdims=True) + EPS)
    return (x * r * g) * (1.0 + scale) + shift


def _gelu(x):
    return jax.nn.gelu(x)


def _dot(a, b):
    return jnp.dot(a, b, preferred_element_type=F32)


def _dot_nt(a, b):
    return lax.dot_general(a, b, (((1,), (1,)), ((), ())), preferred_element_type=F32)


def _ada_kernel(c_ref, w_ref, b_ref, o_ref):
    c = c_ref[...]
    cond = (c * jax.nn.sigmoid(c)).astype(BF16)
    o_ref[...] = _dot(cond, w_ref[...].astype(BF16)) + b_ref[...]


def _ada_call(c, ada_w, ada_b):
    depth, d, n = ada_w.shape
    b = c.shape[0]
    tn = n // 4
    return pl.pallas_call(
        _ada_kernel,
        grid=(depth, n // tn),
        in_specs=[
            pl.BlockSpec((b, d), lambda l, j: (0, 0)),
            pl.BlockSpec((None, d, tn), lambda l, j: (l, 0, j)),
            pl.BlockSpec((None, 1, tn), lambda l, j: (l, 0, j)),
        ],
        out_specs=pl.BlockSpec((None, b, tn), lambda l, j: (l, 0, j)),
        out_shape=jax.ShapeDtypeStruct((depth, b, n), F32),
        compiler_params=pltpu.CompilerParams(
            dimension_semantics=("arbitrary", "arbitrary"), vmem_limit_bytes=VMEM_LIMIT_BYTES),
        name="adaln",
    )(c, ada_w, ada_b.reshape(depth, 1, n))


def _pool_kernel(x_ref, mod_ref, g_ref, win_ref, wgrp_ref, scale_ref, wout_ref, o_ref, ubuf):
    tb, d = x_ref.shape
    grp = d // len(POOL_WINDOWS)
    s = pl.program_id(1)

    @pl.when(s == 0)
    def _():
        ubuf[0:POOL_HALO, :] = jnp.zeros((POOL_HALO, d), F32)

    x = x_ref[...]
    h = _rms_modulate(x, g_ref[...], mod_ref[0], mod_ref[1]).astype(BF16)
    u = _dot(h, win_ref[...])
    ubuf[POOL_HALO:POOL_HALO + tb, :] = u

    pos = s * tb + lax.broadcasted_iota(jnp.int32, (tb, 1), 0)
    t1 = (pos + 1).astype(F32)
    zs = []
    for gi, w in enumerate(POOL_WINDOWS):
        lo = gi * grp
        acc = u[:, lo:lo + grp]
        for k in range(1, w):
            acc = acc + ubuf[POOL_HALO - k:POOL_HALO - k + tb, lo:lo + grp]
        mean = acc / jnp.minimum(t1, float(w))
        p = (mean - u[:, lo:lo + grp]).astype(BF16)
        zs.append(_dot(p, wgrp_ref[gi]))
    z = (jnp.concatenate(zs, axis=-1) * scale_ref[...]).astype(BF16)
    y = _dot(z, wout_ref[...])
    o_ref[...] = x + mod_ref[2] * y
    ubuf[0:POOL_HALO, :] = ubuf[tb:tb + POOL_HALO, :]


def _pool_call(x, mod, g, w_in, w_grp, scale, w_out, *, tb):
    b, s, d = x.shape
    ng, grp = w_grp.shape[0], w_grp.shape[1]
    return pl.pallas_call(
        _pool_kernel,
        grid=(b, s // tb),
        in_specs=[
            pl.BlockSpec((None, tb, d), lambda i, j: (i, j, 0)),
            pl.BlockSpec((None, 6, 1, d), lambda i, j: (i, 0, 0, 0)),
            pl.BlockSpec((1, d), lambda i, j: (0, 0)),
            pl.BlockSpec((d, d), lambda i, j: (0, 0)),
            pl.BlockSpec((ng, grp, grp), lambda i, j: (0, 0, 0)),
            pl.BlockSpec((1, d), lambda i, j: (0, 0)),
            pl.BlockSpec((d, d), lambda i, j: (0, 0)),
        ],
        out_specs=pl.BlockSpec((None, tb, d), lambda i, j: (i, j, 0)),
        out_shape=jax.ShapeDtypeStruct((b, s, d), F32),
        scratch_shapes=[pltpu.VMEM((tb + POOL_HALO, d), F32)],
        compiler_params=pltpu.CompilerParams(
            dimension_semantics=("arbitrary", "arbitrary"), vmem_limit_bytes=VMEM_LIMIT_BYTES),
        name="pool_mixer",
    )(x, mod, g, w_in, w_grp, scale, w_out)


def _conv_kernel(x_ref, mod_ref, g_ref, win_ref, cw_ref, wout_ref, o_ref, zbuf):
    tb, d = x_ref.shape
    s = pl.program_id(1)

    @pl.when(s == 0)
    def _():
        zbuf[0:CONV_HALO, :] = jnp.zeros((CONV_HALO, d), F32)

    x = x_ref[...]
    h = _rms_modulate(x, g_ref[...], mod_ref[0], mod_ref[1]).astype(BF16)
    proj = _dot(h, win_ref[...])
    bg, cg, xt = proj[:, 0:d], proj[:, d:2 * d], proj[:, 2 * d:3 * d]
    z = cg * xt
    zbuf[CONV_HALO:CONV_HALO + tb, :] = z
    zc = z * cw_ref[CONV_WIDTH - 1:CONV_WIDTH, :]
    for k in range(1, CONV_WIDTH):
        zc = zc + zbuf[CONV_HALO - k:CONV_HALO - k + tb, :] * cw_ref[CONV_WIDTH - 1 - k:CONV_WIDTH - k, :]
    y = _dot((bg * zc).astype(BF16), wout_ref[...])
    o_ref[...] = x + mod_ref[2] * y
    zbuf[0:CONV_HALO, :] = zbuf[tb:tb + CONV_HALO, :]


def _conv_call(x, mod, g, w_in, conv_w, w_out, *, tb):
    b, s, d = x.shape
    return pl.pallas_call(
        _conv_kernel,
        grid=(b, s // tb),
        in_specs=[
            pl.BlockSpec((None, tb, d), lambda i, j: (i, j, 0)),
            pl.BlockSpec((None, 6, 1, d), lambda i, j: (i, 0, 0, 0)),
            pl.BlockSpec((1, d), lambda i, j: (0, 0)),
            pl.BlockSpec((d, 3 * d), lambda i, j: (0, 0)),
            pl.BlockSpec((CONV_WIDTH, d), lambda i, j: (0, 0)),
            pl.BlockSpec((d, d), lambda i, j: (0, 0)),
        ],
        out_specs=pl.BlockSpec((None, tb, d), lambda i, j: (i, j, 0)),
        out_shape=jax.ShapeDtypeStruct((b, s, d), F32),
        scratch_shapes=[pltpu.VMEM((tb + CONV_HALO, d), F32)],
        compiler_params=pltpu.CompilerParams(
            dimension_semantics=("arbitrary", "arbitrary"), vmem_limit_bytes=VMEM_LIMIT_BYTES),
        name="conv_mixer",
    )(x, mod, g, w_in, conv_w, w_out)


def _gmlp_kernel(x_ref, mod_ref, g_ref, win_ref, lng_ref, lnb_ref, ws_ref, bs_ref, wout_ref, o_ref):
    tb, d = x_ref.shape
    half = wout_ref.shape[0]
    gdim = half // GMLP_GROUPS

    x = x_ref[...]
    h = _rms_modulate(x, g_ref[...], mod_ref[0], mod_ref[1]).astype(BF16)
    z = _gelu(_dot(h, win_ref[...]))
    u, v = z[:, 0:half], z[:, half:2 * half]
    mu = jnp.mean(v, axis=-1, keepdims=True)
    vc = v - mu
    var = jnp.mean(vc * vc, axis=-1, keepdims=True)
    v = ((vc * lax.rsqrt(var + EPS)) * lng_ref[...] + lnb_ref[...]).astype(BF16)

    row = lax.broadcasted_iota(jnp.int32, (GMLP_BLOCK, GMLP_BLOCK), 0) // CHUNK
    col = lax.broadcasted_iota(jnp.int32, (GMLP_BLOCK, GMLP_BLOCK), 1) // CHUNK
    causal = col <= row
    blocks = []
    for n in range(tb // GMLP_BLOCK):
        r0 = n * GMLP_BLOCK
        cols = []
        for gi in range(GMLP_GROUPS):
            ws = jnp.where(causal, ws_ref[gi], 0.0).astype(BF16)
            vs = _dot(ws, v[r0:r0 + GMLP_BLOCK, gi * gdim:(gi + 1) * gdim]) + bs_ref[gi]
            cols.append(vs)
        blocks.append(jnp.concatenate(cols, axis=-1))
    vs = jnp.concatenate(blocks, axis=0)
    y = _dot((u * vs).astype(BF16), wout_ref[...])
    o_ref[...] = x + mod_ref[2] * y


def _gmlp_call(x, mod, g, w_in, ln_g, ln_b, w_s, b_s, w_out, *, tb):
    b, s, d = x.shape
    half = w_out.shape[0]
    nblk = s // tb
    const = dict(pipeline_mode=pl.Buffered(1))
    return pl.pallas_call(
        _gmlp_kernel,
        grid=(b, nblk),
        in_specs=[
            pl.BlockSpec((None, tb, d), lambda i, j: (i, j, 0)),
            pl.BlockSpec((None, 6, 1, d), lambda i, j: (i, 0, 0, 0)),
            pl.BlockSpec((1, d), lambda i, j: (0, 0)),
            pl.BlockSpec((d, 2 * half), lambda i, j: (0, 0), **const),
            pl.BlockSpec((1, half), lambda i, j: (0, 0)),
            pl.BlockSpec((1, half), lambda i, j: (0, 0)),
            pl.BlockSpec((GMLP_GROUPS, GMLP_BLOCK, GMLP_BLOCK), lambda i, j: (0, 0, 0)),
            pl.BlockSpec((GMLP_GROUPS, GMLP_BLOCK, 1), lambda i, j: (0, 0, 0)),
            pl.BlockSpec((half, d), lambda i, j: (0, 0), **const),
        ],
        out_specs=pl.BlockSpec((None, tb, d), lambda i, j: (i, j, 0)),
        out_shape=jax.ShapeDtypeStruct((b, s, d), F32),
        compiler_params=pltpu.CompilerParams(
            dimension_semantics=("arbitrary", "arbitrary"), vmem_limit_bytes=VMEM_LIMIT_BYTES),
        name="gmlp_mixer",
    )(x, mod, g, w_in, ln_g, ln_b, w_s, b_s, w_out)


def _sorted_top(s, k):
    rows = lax.broadcasted_iota(jnp.int32, (k, s.shape[1]), 0)
    out = jnp.zeros((k, s.shape[1]), F32)
    for i in range(k):
        m = jnp.max(s, axis=0, keepdims=True)
        out = jnp.where(rows == i, m, out)
        s = jnp.where(s == m, -jnp.inf, s)
    return out


def _route_tile(s0, s1):
    k = PEER_TOPK
    a = _sorted_top(s0, k)
    b = _sorted_top(s1, k)
    parts = [a[0:1] + b]
    for r in range(1, 8):
        parts.append(a[r:r + 1] + b[0:8])
    parts.append(a[8:16] + b[0:1])
    cand = jnp.concatenate(parts, axis=0)
    top = cand[0:1]
    work = cand
    tau = top
    for _ in range(k):
        tau = jnp.max(work, axis=0, keepdims=True)
        work = jnp.where(work == tau, -jnp.inf, work)
    z = jnp.sum(jnp.where(cand >= tau, jnp.exp(cand - top), 0.0), axis=0, keepdims=True)
    a0 = jnp.exp(s0 - a[0:1]) / z
    e1 = jnp.exp(s1 - b[0:1])
    return tau, a0, e1


def _peer_kernel(x_ref, mod_ref, g_ref, fg_ref, wq_ref, keys_ref, u_ref, vt_ref, o_ref,
                 h_scr, q_scr, s0_scr, s1_scr, a0_scr, e1_scr, tau_scr, acc_scr, p_scr,
                 *, sub, final_norm):
    tb, d = x_ref.shape
    ec = u_ref.shape[0]
    nk = keys_ref.shape[2]
    dh = keys_ref.shape[3]
    heads = keys_ref.shape[0]
    c = pl.program_id(1)
    nc = pl.num_programs(1)
    rows_per_step = ec // nk
    n_sub = tb // sub
    n_lt = sub // LANES

    @pl.when(c == 0)
    def _prologue():
        x = x_ref[...]
        h = _rms_modulate(x, g_ref[...], mod_ref[3], mod_ref[4]).astype(BF16)
        h_scr[...] = h
        q_scr[...] = _dot_nt(wq_ref[...], h).astype(BF16)
        acc_scr[...] = jnp.zeros_like(acc_scr)

        def route(it, carry):
            hd = it // (tb // LANES)
            lt = it % (tb // LANES)
            lanes = pl.ds(pl.multiple_of(lt * LANES, LANES), LANES)
            q0 = q_scr[pl.ds(pl.multiple_of(hd * 2 * dh, dh), dh), lanes]
            q1 = q_scr[pl.ds(pl.multiple_of(hd * 2 * dh + dh, dh), dh), lanes]
            s0 = _dot(keys_ref[hd, 0], q0)
            s1 = _dot(keys_ref[hd, 1], q1)
            tau, a0, e1 = _route_tile(s0, s1)
            s0_scr[hd, :, :, lanes] = s0.reshape(nk // 8, 8, LANES)
            s1_scr[hd, :, lanes] = s1
            a0_scr[hd, :, :, lanes] = a0.reshape(nk // 8, 8, LANES)
            e1_scr[hd, :, lanes] = e1
            tau_scr[hd, :, lanes] = jnp.broadcast_to(tau, (8, LANES))
            return carry

        lax.fori_loop(0, heads * (tb // LANES), route, 0)

    def dense(n, carry):
        tok = pl.ds(pl.multiple_of(n * sub, sub), sub)
        yt = _dot_nt(u_ref[...], h_scr[tok, :])
        for ii in range(rows_per_step):
            grp = c * (rows_per_step // 8) + ii // 8
            sl = ii % 8
            for lt in range(n_lt):
                lanes = pl.ds(pl.multiple_of(n * sub + lt * LANES, LANES), LANES)
                w = jnp.zeros((nk, LANES), F32)
                for hd in range(heads):
                    s0r = s0_scr[hd, grp, sl:sl + 1, lanes]
                    a0r = a0_scr[hd, grp, sl:sl + 1, lanes]
                    sel = (s0r + s1_scr[hd, :, lanes]) >= tau_scr[hd, 0:1, lanes]
                    w = w + jnp.where(sel, a0r * e1_scr[hd, :, lanes], 0.0)
                act = _gelu(yt[ii * nk:(ii + 1) * nk, lt * LANES:(lt + 1) * LANES])
                p_scr[ii * nk:(ii + 1) * nk, lt * LANES:(lt + 1) * LANES] = (w * act).astype(BF16)
        acc_scr[:, tok] += _dot(vt_ref[...], p_scr[...])
        return carry

    lax.fori_loop(0, n_sub, dense, 0)

    @pl.when(c == nc - 1)
    def _epilogue():
        x = x_ref[...]
        out = x + mod_ref[5] * acc_scr[...].T
        if final_norm:
            r = lax.rsqrt(jnp.mean(out * out, axis=-1, keepdims=True) + EPS)
            out = out * r * fg_ref[...]
        o_ref[...] = out


def _peer_call(x2, mod, g, final_g, wq_t, keys, u_tab, vt_tab, *, layer, seq, tb, ec, sub,
               final_norm):
    t, d = x2.shape
    n_exp = u_tab.shape[1]
    heads, _, nk, dh = keys.shape[1:]
    tiles_per_seq = seq // tb
    assert ec % (8 * nk) == 0 and tb % sub == 0 and sub % LANES == 0 and seq % tb == 0
    kern = functools.partial(_peer_kernel, sub=sub, final_norm=final_norm)
    return pl.pallas_call(
        kern,
        grid=(t // tb, n_exp // ec),
        in_specs=[
            pl.BlockSpec((tb, d), lambda i, c: (i, 0)),
            pl.BlockSpec((None, 6, 1, d), lambda i, c: (i // tiles_per_seq, 0, 0, 0)),
            pl.BlockSpec((1, d), lambda i, c: (0, 0)),
            pl.BlockSpec((1, d), lambda i, c: (0, 0)),
            pl.BlockSpec((None, heads * 2 * dh, d), lambda i, c: (layer, 0, 0)),
            pl.BlockSpec((None, heads, 2, nk, dh), lambda i, c: (layer, 0, 0, 0, 0)),
            pl.BlockSpec((None, ec, d), lambda i, c: (layer, c, 0)),
            pl.BlockSpec((None, d, ec), lambda i, c: (layer, 0, c)),
        ],
        out_specs=pl.BlockSpec((tb, d), lambda i, c: (i, 0)),
        out_shape=jax.ShapeDtypeStruct((t, d), F32),
        scratch_shapes=[
            pltpu.VMEM((tb, d), BF16),
            pltpu.VMEM((heads * 2 * dh, tb), BF16),
            pltpu.VMEM((heads, nk // 8, 8, tb), F32),
            pltpu.VMEM((heads, nk, tb), F32),
            pltpu.VMEM((heads, nk // 8, 8, tb), F32),
            pltpu.VMEM((heads, nk, tb), F32),
            pltpu.VMEM((heads, 8, tb), F32),
            pltpu.VMEM((d, tb), F32),
            pltpu.VMEM((ec, sub), BF16),
        ],
        compiler_params=pltpu.CompilerParams(
            dimension_semantics=("arbitrary", "arbitrary"), vmem_limit_bytes=VMEM_LIMIT_BYTES),
        name="peer",
    )(x2, mod, g, final_g, wq_t, keys, u_tab, vt_tab)


def _tiles(seq, n_exp):
    tb_mix = min(seq, 512)
    tb_gmlp = min(seq, 256)
    tb_peer = min(seq, 512)
    ec = min(n_exp, 1024)
    sub = min(tb_peer, 256)
    return tb_mix, tb_gmlp, tb_peer, ec, sub


def kernel(x, c, ada_w, ada_b, norm_g, pool_w_in, pool_w_grp, pool_scale, pool_w_out, conv_w_in, conv_w, conv_w_out, gmlp_w_in, gmlp_ln_g, gmlp_ln_b, gmlp_w_s, gmlp_b_s, gmlp_w_out, peer_w_q, peer_keys, peer_u, peer_v, final_g):
    b, seq, d = x.shape
    depth = ada_w.shape[0]
    n_exp = peer_u.shape[1]
    tb_mix, tb_gmlp, tb_peer, ec, sub = _tiles(seq, n_exp)

    mod = _ada_call(c, ada_w, ada_b).reshape(depth, b, 6, 1, d)

    wq_t = jnp.swapaxes(peer_w_q, 1, 2).astype(BF16)
    keys = peer_keys.astype(BF16)
    u_tab = peer_u.astype(BF16)
    vt_tab = jnp.swapaxes(peer_v, 1, 2).astype(BF16)
    fg = final_g.reshape(1, d)

    for i in range(depth):
        kind, j = i % N_MIXERS, i // N_MIXERS
        g1 = norm_g[i, 0].reshape(1, d)
        if kind == 0:
            x = _pool_call(x, mod[i], g1, pool_w_in[j].astype(BF16), pool_w_grp[j].astype(BF16),
                           pool_scale[j].reshape(1, d), pool_w_out[j].astype(BF16), tb=tb_mix)
        elif kind == 1:
            x = _conv_call(x, mod[i], g1, conv_w_in[j].astype(BF16), conv_w[j],
                           conv_w_out[j].astype(BF16), tb=tb_mix)
        else:
            half = gmlp_w_out.shape[1]
            x = _gmlp_call(x, mod[i], g1, gmlp_w_in[j].astype(BF16), gmlp_ln_g[j].reshape(1, half),
                           gmlp_ln_b[j].reshape(1, half), gmlp_w_s[j],
                           gmlp_b_s[j].reshape(GMLP_GROUPS, GMLP_BLOCK, 1),
                           gmlp_w_out[j].astype(BF16), tb=tb_gmlp)
        x = _peer_call(x.reshape(b * seq, d), mod[i], norm_g[i, 1].reshape(1, d), fg, wq_t, keys,
                       u_tab, vt_tab, layer=i, seq=seq, tb=tb_peer, ec=ec, sub=sub,
                       final_norm=(i == depth - 1)).reshape(b, seq, d)
    return x
```

```python
import functools

import jax
import jax.numpy as jnp
from jax import lax
from jax.experimental import pallas as pl
from jax.experimental.pallas import tpu as pltpu

EPS = 1e-6
CHUNK = 64
POOL_WINDOWS = (2, 4, 8, 16)
POOL_HALO = 16
CONV_WIDTH = 3
CONV_HALO = 8
GMLP_BLOCK = 128
GMLP_GROUPS = 8
PEER_HEADS = 8
PEER_TOPK = 16
N_MIXERS = 3

LANES = 128
SUBLANES = 8
PACKED_ROWS = 16
VMEM_LIMIT_BYTES = 56 * 1024 * 1024

BF16 = jnp.bfloat16
F32 = jnp.float32
U32 = jnp.uint32


def _rms_modulate(x, g, shift, scale):
    r = lax.rsqrt(jnp.mean(x * x, axis=-1, keepdims=True) + EPS)
    return (x * r * g) * (1.0 + scale) + shift


_GELU_K = 0.7978845608028654


def _gelu(x):
    inner = x * (_GELU_K + (_GELU_K * 0.044715) * (x * x))
    hx = 0.5 * x
    return hx + hx * jnp.tanh(inner)


def _dot(a, b):
    return jnp.dot(a, b, preferred_element_type=F32)


def _dot_nt(a, b):
    return lax.dot_general(a, b, (((1,), (1,)), ((), ())), preferred_element_type=F32)


def _ada_kernel(c_ref, w_ref, b_ref, o_ref):
    c = c_ref[...]
    cond = (c * jax.nn.sigmoid(c)).astype(BF16)
    o_ref[...] = _dot(cond, w_ref[...].astype(BF16)) + b_ref[...]


def _ada_call(c, ada_w, ada_b):
    depth, d, n = ada_w.shape
    b = c.shape[0]
    tn = n // 4
    return pl.pallas_call(
        _ada_kernel,
        grid=(depth, n // tn),
        in_specs=[
            pl.BlockSpec((b, d), lambda l, j: (0, 0)),
            pl.BlockSpec((None, d, tn), lambda l, j: (l, 0, j)),
            pl.BlockSpec((None, 1, tn), lambda l, j: (l, 0, j)),
        ],
        out_specs=pl.BlockSpec((None, b, tn), lambda l, j: (l, 0, j)),
        out_shape=jax.ShapeDtypeStruct((depth, b, n), F32),
        compiler_params=pltpu.CompilerParams(
            dimension_semantics=("arbitrary", "arbitrary"), vmem_limit_bytes=VMEM_LIMIT_BYTES),
        name="adaln",
    )(c, ada_w, ada_b.reshape(depth, 1, n))


def _pool_kernel(x_ref, mod_ref, g_ref, win_ref, wgrp_ref, scale_ref, wout_ref, o_ref, ubuf):
    tb, d = x_ref.shape
    grp = d // len(POOL_WINDOWS)
    s = pl.program_id(1)

    @pl.when(s == 0)
    def _():
        ubuf[0:POOL_HALO, :] = jnp.zeros((POOL_HALO, d), F32)

    x = x_ref[...]
    h = _rms_modulate(x, g_ref[...], mod_ref[0], mod_ref[1]).astype(BF16)
    u = _dot(h, win_ref[...])
    ubuf[POOL_HALO:POOL_HALO + tb, :] = u

    pos = s * tb + lax.broadcasted_iota(jnp.int32, (tb, 1), 0)
    t1 = (pos + 1).astype(F32)
    zs = []
    for gi, w in enumerate(POOL_WINDOWS):
        lo = gi * grp
        acc = u[:, lo:lo + grp]
        for k in range(1, w):
            acc = acc + ubuf[POOL_HALO - k:POOL_HALO - k + tb, lo:lo + grp]
        mean = acc / jnp.minimum(t1, float(w))
        p = (mean - u[:, lo:lo + grp]).astype(BF16)
        zs.append(_dot(p, wgrp_ref[gi]))
    z = (jnp.concatenate(zs, axis=-1) * scale_ref[...]).astype(BF16)
    y = _dot(z, wout_ref[...])
    o_ref[...] = x + mod_ref[2] * y
    ubuf[0:POOL_HALO, :] = ubuf[tb:tb + POOL_HALO, :]


def _pool_call(x, mod, g, w_in, w_grp, scale, w_out, *, tb):
    b, s, d = x.shape
    ng, grp = w_grp.shape[0], w_grp.shape[1]
    return pl.pallas_call(
        _pool_kernel,
        grid=(b, s // tb),
        in_specs=[
            pl.BlockSpec((None, tb, d), lambda i, j: (i, j, 0)),
            pl.BlockSpec((None, 6, 1, d), lambda i, j: (i, 0, 0, 0)),
            pl.BlockSpec((1, d), lambda i, j: (0, 0)),
            pl.BlockSpec((d, d), lambda i, j: (0, 0)),
            pl.BlockSpec((ng, grp, grp), lambda i, j: (0, 0, 0)),
            pl.BlockSpec((1, d), lambda i, j: (0, 0)),
            pl.BlockSpec((d, d), lambda i, j: (0, 0)),
        ],
        out_specs=pl.BlockSpec((None, tb, d), lambda i, j: (i, j, 0)),
        out_shape=jax.ShapeDtypeStruct((b, s, d), F32),
        scratch_shapes=[pltpu.VMEM((tb + POOL_HALO, d), F32)],
        compiler_params=pltpu.CompilerParams(
            dimension_semantics=("arbitrary", "arbitrary"), vmem_limit_bytes=VMEM_LIMIT_BYTES),
        name="pool_mixer",
    )(x, mod, g, w_in, w_grp, scale, w_out)


def _conv_kernel(x_ref, mod_ref, g_ref, win_ref, cw_ref, wout_ref, o_ref, zbuf):
    tb, d = x_ref.shape
    s = pl.program_id(1)

    @pl.when(s == 0)
    def _():
        zbuf[0:CONV_HALO, :] = jnp.zeros((CONV_HALO, d), F32)

    x = x_ref[...]
    h = _rms_modulate(x, g_ref[...], mod_ref[0], mod_ref[1]).astype(BF16)
    proj = _dot(h, win_ref[...])
    bg, cg, xt = proj[:, 0:d], proj[:, d:2 * d], proj[:, 2 * d:3 * d]
    z = cg * xt
    zbuf[CONV_HALO:CONV_HALO + tb, :] = z
    zc = z * cw_ref[CONV_WIDTH - 1:CONV_WIDTH, :]
    for k in range(1, CONV_WIDTH):
        zc = zc + zbuf[CONV_HALO - k:CONV_HALO - k + tb, :] * cw_ref[CONV_WIDTH - 1 - k:CONV_WIDTH - k, :]
    y = _dot((bg * zc).astype(BF16), wout_ref[...])
    o_ref[...] = x + mod_ref[2] * y
    zbuf[0:CONV_HALO, :] = zbuf[tb:tb + CONV_HALO, :]


def _conv_call(x, mod, g, w_in, conv_w, w_out, *, tb):
    b, s, d = x.shape
    return pl.pallas_call(
        _conv_kernel,
        grid=(b, s // tb),
        in_specs=[
            pl.BlockSpec((None, tb, d), lambda i, j: (i, j, 0)),
            pl.BlockSpec((None, 6, 1, d), lambda i, j: (i, 0, 0, 0)),
            pl.BlockSpec((1, d), lambda i, j: (0, 0)),
            pl.BlockSpec((d, 3 * d), lambda i, j: (0, 0)),
            pl.BlockSpec((CONV_WIDTH, d), lambda i, j: (0, 0)),
            pl.BlockSpec((d, d), lambda i, j: (0, 0)),
        ],
        out_specs=pl.BlockSpec((None, tb, d), lambda i, j: (i, j, 0)),
        out_shape=jax.ShapeDtypeStruct((b, s, d), F32),
        scratch_shapes=[pltpu.VMEM((tb + CONV_HALO, d), F32)],
        compiler_params=pltpu.CompilerParams(
            dimension_semantics=("arbitrary", "arbitrary"), vmem_limit_bytes=VMEM_LIMIT_BYTES),
        name="conv_mixer",
    )(x, mod, g, w_in, conv_w, w_out)


def _gmlp_kernel(x_ref, mod_ref, g_ref, win_ref, lng_ref, lnb_ref, ws_ref, bs_ref, wout_ref, o_ref):
    tb, d = x_ref.shape
    half = wout_ref.shape[0]
    gdim = half // GMLP_GROUPS

    x = x_ref[...]
    h = _rms_modulate(x, g_ref[...], mod_ref[0], mod_ref[1]).astype(BF16)
    z = _gelu(_dot(h, win_ref[...]))
    u, v = z[:, 0:half], z[:, half:2 * half]
    mu = jnp.mean(v, axis=-1, keepdims=True)
    vc = v - mu
    var = jnp.mean(vc * vc, axis=-1, keepdims=True)
    v = ((vc * lax.rsqrt(var + EPS)) * lng_ref[...] + lnb_ref[...]).astype(BF16)

    row = lax.broadcasted_iota(jnp.int32, (GMLP_BLOCK, GMLP_BLOCK), 0) // CHUNK
    col = lax.broadcasted_iota(jnp.int32, (GMLP_BLOCK, GMLP_BLOCK), 1) // CHUNK
    causal = col <= row
    blocks = []
    for n in range(tb // GMLP_BLOCK):
        r0 = n * GMLP_BLOCK
        cols = []
        for gi in range(GMLP_GROUPS):
            ws = jnp.where(causal, ws_ref[gi], 0.0).astype(BF16)
            vs = _dot(ws, v[r0:r0 + GMLP_BLOCK, gi * gdim:(gi + 1) * gdim]) + bs_ref[gi]
            cols.append(vs)
        blocks.append(jnp.concatenate(cols, axis=-1))
    vs = jnp.concatenate(blocks, axis=0)
    y = _dot((u * vs).astype(BF16), wout_ref[...])
    o_ref[...] = x + mod_ref[2] * y


def _gmlp_call(x, mod, g, w_in, ln_g, ln_b, w_s, b_s, w_out, *, tb):
    b, s, d = x.shape
    half = w_out.shape[0]
    nblk = s // tb
    const = dict(pipeline_mode=pl.Buffered(1))
    return pl.pallas_call(
        _gmlp_kernel,
        grid=(b, nblk),
        in_specs=[
            pl.BlockSpec((None, tb, d), lambda i, j: (i, j, 0)),
            pl.BlockSpec((None, 6, 1, d), lambda i, j: (i, 0, 0, 0)),
            pl.BlockSpec((1, d), lambda i, j: (0, 0)),
            pl.BlockSpec((d, 2 * half), lambda i, j: (0, 0), **const),
            pl.BlockSpec((1, half), lambda i, j: (0, 0)),
            pl.BlockSpec((1, half), lambda i, j: (0, 0)),
            pl.BlockSpec((GMLP_GROUPS, GMLP_BLOCK, GMLP_BLOCK), lambda i, j: (0, 0, 0)),
            pl.BlockSpec((GMLP_GROUPS, GMLP_BLOCK, 1), lambda i, j: (0, 0, 0)),
            pl.BlockSpec((half, d), lambda i, j: (0, 0), **const),
        ],
        out_specs=pl.BlockSpec((None, tb, d), lambda i, j: (i, j, 0)),
        out_shape=jax.ShapeDtypeStruct((b, s, d), F32),
        compiler_params=pltpu.CompilerParams(
            dimension_semantics=("arbitrary", "arbitrary"), vmem_limit_bytes=VMEM_LIMIT_BYTES),
        name="gmlp_mixer",
    )(x, mod, g, w_in, ln_g, ln_b, w_s, b_s, w_out)


def _sorted_top(s, k, want_rank):
    rows = lax.broadcasted_iota(jnp.int32, (k, s.shape[1]), 0)
    out = jnp.zeros((k, s.shape[1]), F32)
    rank = jnp.full(s.shape, float(k), F32) if want_rank else None
    for i in range(k):
        m = jnp.max(s, axis=0, keepdims=True)
        eq = s == m
        out = jnp.where(rows == i, m, out)
        if want_rank:
            rank = jnp.where(eq, float(i), rank)
        s = jnp.where(eq, -jnp.inf, s)
    return out, rank


def _dup_bf16_bits(x):
    hi = pltpu.bitcast(x, U32) & jnp.uint32(0xFFFF0000)
    return hi | (hi >> 16)


def _route_tile(s0, s1):
    k = PEER_TOPK
    a, _ = _sorted_top(s0, k, False)
    b, rank1 = _sorted_top(s1, k, True)
    parts = [a[0:1] + b]
    for r in range(1, 8):
        parts.append(a[r:r + 1] + b[0:8])
    parts.append(a[8:16] + b[0:1])
    cand = jnp.concatenate(parts, axis=0)
    top = cand[0:1]
    work = cand
    tau = top
    for _ in range(k):
        tau = jnp.max(work, axis=0, keepdims=True)
        work = jnp.where(work == tau, -jnp.inf, work)
    z = jnp.sum(jnp.where(cand >= tau, jnp.exp(cand - top), 0.0), axis=0, keepdims=True)

    cnt_r = jnp.zeros(a.shape, F32)
    for c in range(k):
        cnt_r = cnt_r + jnp.where(a + b[c:c + 1] >= tau, 1.0, 0.0)
    cnt = jnp.zeros(s0.shape, F32)
    for r in range(k):
        cnt = jnp.where(s0 == a[r:r + 1], cnt_r[r:r + 1], cnt)

    a0 = (jnp.exp(s0 - a[0:1]) / z).astype(BF16).astype(F32)
    e1 = jnp.exp(s1 - b[0:1]).astype(BF16)
    return rank1.astype(BF16), e1, _dup_bf16_bits(cnt), _dup_bf16_bits(a0)


def _peer_kernel(x_ref, mod_ref, g_ref, fg_ref, wq_ref, keys_ref, u_ref, vt_ref, o_ref,
                 h_scr, q_scr, rank_scr, e1_scr, cnt_scr, a0_scr, y0_scr, y1_scr, p0_scr, p1_scr,
                 acc_scr,
                 *, route_lanes, mxu_rows, final_norm):
    tb, d = x_ref.shape
    ec = u_ref.shape[0]
    heads, _, nk, dh = keys_ref.shape
    c = pl.program_id(1)
    nc = pl.num_programs(1) - 2
    rows_per_step = ec // nk
    n_lt = tb // LANES
    n_rt = tb // route_lanes

    y_bufs = (y0_scr, y1_scr)
    p_bufs = (p0_scr, p1_scr)

    def stage_a(slot):
        def item(m):
            rows = slice(m * mxu_rows, (m + 1) * mxu_rows)
            y_bufs[slot][rows, :] = _dot_nt(u_ref[rows, :], h_scr[...])
        return [functools.partial(item, m) for m in range(ec // mxu_rows)]

    def stage_b(slot):
        cb = c - 1
        y_scr, p_scr = y_bufs[slot], p_bufs[slot]

        def item(ii, lt):
            grp = cb * (rows_per_step // SUBLANES) + ii // SUBLANES
            sl = ii % SUBLANES
            lanes = slice(lt * LANES, (lt + 1) * LANES)
            cnt_rows, a0_rows = [], []
            for hd in range(heads):
                cw = jnp.broadcast_to(cnt_scr[hd, grp, sl:sl + 1, lanes], (SUBLANES, LANES))
                aw = jnp.broadcast_to(a0_scr[hd, grp, sl:sl + 1, lanes], (SUBLANES, LANES))
                cnt_rows.append(pltpu.bitcast(cw, BF16))
                a0_rows.append(pltpu.bitcast(aw, BF16))
            for jg in range(nk // PACKED_ROWS):
                keys_j = slice(jg * PACKED_ROWS, (jg + 1) * PACKED_ROWS)
                w = jnp.zeros((PACKED_ROWS, LANES), BF16)
                for hd in range(heads):
                    sel = rank_scr[hd, keys_j, lanes] < cnt_rows[hd]
                    w = w + jnp.where(sel, a0_rows[hd] * e1_scr[hd, keys_j, lanes], 0.0)
                r0 = ii * nk + jg * PACKED_ROWS
                act = _gelu(y_scr[r0:r0 + PACKED_ROWS, lanes]).astype(BF16)
                p_scr[r0:r0 + PACKED_ROWS, lanes] = w * act

        return [functools.partial(item, ii, lt) for ii in range(rows_per_step)
                for lt in range(n_lt)]

    def stage_c(slot):
        def item(m):
            rows = slice(m * mxu_rows, (m + 1) * mxu_rows)
            acc_scr[rows, :] += _dot(vt_ref[rows, :], p_bufs[slot][...])
        return [functools.partial(item, m) for m in range(d // mxu_rows)]

    def run(mxu_items, valu_items):
        n_m, n_v = len(mxu_items), len(valu_items)
        if n_m == 0 or n_v == 0:
            for f in mxu_items + valu_items:
                f()
            return
        done_v = 0
        for k, f in enumerate(mxu_items):
            f()
            upto = ((k + 1) * n_v) // n_m
            for g in valu_items[done_v:upto]:
                g()
            done_v = upto

    def interleave(a, b):
        out = []
        for k in range(max(len(a), len(b))):
            out += a[k:k + 1] + b[k:k + 1]
        return out

    @pl.when(c == 0)
    def _first():
        x = x_ref[...]
        h = _rms_modulate(x, g_ref[...], mod_ref[3], mod_ref[4]).astype(BF16)
        h_scr[...] = h
        q_scr[...] = _dot_nt(wq_ref[...], h).astype(BF16)
        acc_scr[...] = jnp.zeros_like(acc_scr)

        def route(it, carry):
            hd = it // n_rt
            lanes = pl.ds(pl.multiple_of((it % n_rt) * route_lanes, route_lanes), route_lanes)
            q0 = q_scr[pl.ds(pl.multiple_of(hd * 2 * dh, dh), dh), lanes]
            q1 = q_scr[pl.ds(pl.multiple_of(hd * 2 * dh + dh, dh), dh), lanes]
            s0 = _dot(keys_ref[hd, 0], q0)
            s1 = _dot(keys_ref[hd, 1], q1)
            rank1, e1, cnt, a0 = _route_tile(s0, s1)
            rank_scr[hd, :, lanes] = rank1
            e1_scr[hd, :, lanes] = e1
            cnt_scr[hd, :, :, lanes] = cnt.reshape(nk // SUBLANES, SUBLANES, route_lanes)
            a0_scr[hd, :, :, lanes] = a0.reshape(nk // SUBLANES, SUBLANES, route_lanes)
            return carry

        lax.fori_loop(0, heads * n_rt, route, 0)
        run(stage_a(0), [])

    @pl.when(c == 1)
    def _fill():
        run(stage_a(1), stage_b(0))

    steady = jnp.logical_and(c >= 2, c < nc)

    @pl.when(jnp.logical_and(steady, c % 2 == 0))
    def _steady_even():
        run(interleave(stage_a(0), stage_c(0)), stage_b(1))

    @pl.when(jnp.logical_and(steady, c % 2 == 1))
    def _steady_odd():
        run(interleave(stage_a(1), stage_c(1)), stage_b(0))

    @pl.when(c == nc)
    def _drain():
        run(stage_c(0), stage_b(1))

    @pl.when(c == nc + 1)
    def _last():
        run(stage_c(1), [])
        x = x_ref[...]
        out = x + mod_ref[5] * acc_scr[...].T
        if final_norm:
            r = lax.rsqrt(jnp.mean(out * out, axis=-1, keepdims=True) + EPS)
            out = out * r * fg_ref[...]
        o_ref[...] = out


def _peer_call(x2, mod, g, final_g, wq_t, keys, u_tab, vt_tab, *, layer, seq, tb, ec, route_lanes, mxu_rows,
               final_norm):
    t, d = x2.shape
    n_exp = u_tab.shape[1]
    heads, _, nk, dh = keys.shape[1:]
    tiles_per_seq = seq // tb
    nc = n_exp // ec
    assert ec % (SUBLANES * nk) == 0 and nk % PACKED_ROWS == 0 and nc >= 4 and nc % 2 == 0
    assert seq % tb == 0 and tb % route_lanes == 0 and route_lanes % LANES == 0
    kern = functools.partial(_peer_kernel, route_lanes=route_lanes, mxu_rows=mxu_rows,
                             final_norm=final_norm)
    return pl.pallas_call(
        kern,
        grid=(t // tb, nc + 2),
        in_specs=[
            pl.BlockSpec((tb, d), lambda i, c: (i, 0)),
            pl.BlockSpec((None, 6, 1, d), lambda i, c: (i // tiles_per_seq, 0, 0, 0)),
            pl.BlockSpec((1, d), lambda i, c: (0, 0)),
            pl.BlockSpec((1, d), lambda i, c: (0, 0)),
            pl.BlockSpec((None, heads * 2 * dh, d), lambda i, c: (layer, 0, 0)),
            pl.BlockSpec((None, heads, 2, nk, dh), lambda i, c: (layer, 0, 0, 0, 0)),
            pl.BlockSpec((None, ec, d), lambda i, c: (layer, jnp.minimum(c, nc - 1), 0)),
            pl.BlockSpec((None, d, ec), lambda i, c: (layer, 0, jnp.clip(c - 2, 0, nc - 1))),
        ],
        out_specs=pl.BlockSpec((tb, d), lambda i, c: (i, 0)),
        out_shape=jax.ShapeDtypeStruct((t, d), F32),
        scratch_shapes=[
            pltpu.VMEM((tb, d), BF16),
            pltpu.VMEM((heads * 2 * dh, tb), BF16),
            pltpu.VMEM((heads, nk, tb), BF16),
            pltpu.VMEM((heads, nk, tb), BF16),
            pltpu.VMEM((heads, nk // SUBLANES, SUBLANES, tb), U32),
            pltpu.VMEM((heads, nk // SUBLANES, SUBLANES, tb), U32),
            pltpu.VMEM((ec, tb), F32),
            pltpu.VMEM((ec, tb), F32),
            pltpu.VMEM((ec, tb), BF16),
            pltpu.VMEM((ec, tb), BF16),
            pltpu.VMEM((d, tb), F32),
        ],
        compiler_params=pltpu.CompilerParams(
            dimension_semantics=("arbitrary", "arbitrary"), vmem_limit_bytes=VMEM_LIMIT_BYTES),
        name="peer",
    )(x2, mod, g, final_g, wq_t, keys, u_tab, vt_tab)


def _tiles(seq, n_exp):
    tb_mix = min(seq, 512)
    tb_gmlp = min(seq, 256)
    tb_peer = min(seq, 512)
    ec = min(n_exp // 4, 1024)
    route_lanes = min(tb_peer, 256)
    mxu_rows = 256
    return tb_mix, tb_gmlp, tb_peer, ec, route_lanes, mxu_rows


def kernel(x, c, ada_w, ada_b, norm_g, pool_w_in, pool_w_grp, pool_scale, pool_w_out, conv_w_in, conv_w, conv_w_out, gmlp_w_in, gmlp_ln_g, gmlp_ln_b, gmlp_w_s, gmlp_b_s, gmlp_w_out, peer_w_q, peer_keys, peer_u, peer_v, final_g):
    b, seq, d = x.shape
    depth = ada_w.shape[0]
    n_exp = peer_u.shape[1]
    tb_mix, tb_gmlp, tb_peer, ec, route_lanes, mxu_rows = _tiles(seq, n_exp)

    mod = _ada_call(c, ada_w, ada_b).reshape(depth, b, 6, 1, d)

    wq_t = jnp.swapaxes(peer_w_q, 1, 2).astype(BF16)
    keys = peer_keys.astype(BF16)
    u_tab = peer_u.astype(BF16)
    vt_tab = jnp.swapaxes(peer_v, 1, 2).astype(BF16)
    fg = final_g.reshape(1, d)

    for i in range(depth):
        kind, j = i % N_MIXERS, i // N_MIXERS
        g1 = norm_g[i, 0].reshape(1, d)
        if kind == 0:
            x = _pool_call(x, mod[i], g1, pool_w_in[j].astype(BF16), pool_w_grp[j].astype(BF16),
                           pool_scale[j].reshape(1, d), pool_w_out[j].astype(BF16), tb=tb_mix)
        elif kind == 1:
            x = _conv_call(x, mod[i], g1, conv_w_in[j].astype(BF16), conv_w[j],
                           conv_w_out[j].astype(BF16), tb=tb_mix)
        else:
            half = gmlp_w_out.shape[1]
            x = _gmlp_call(x, mod[i], g1, gmlp_w_in[j].astype(BF16), gmlp_ln_g[j].reshape(1, half),
                           gmlp_ln_b[j].reshape(1, half), gmlp_w_s[j],
                           gmlp_b_s[j].reshape(GMLP_GROUPS, GMLP_BLOCK, 1),
                           gmlp_w_out[j].astype(BF16), tb=tb_gmlp)
        x = _peer_call(x.reshape(b * seq, d), mod[i], norm_g[i, 1].reshape(1, d), fg, wq_t, keys,
                       u_tab, vt_tab, layer=i, seq=seq, tb=tb_peer, ec=ec, route_lanes=route_lanes, mxu_rows=mxu_rows,
                       final_norm=(i == depth - 1)).reshape(b, seq, d)
    return x
```

```python
import functools

import jax
import jax.numpy as jnp
from jax import lax
from jax.experimental import pallas as pl
from jax.experimental.pallas import tpu as pltpu

EPS = 1e-6
CHUNK = 64
POOL_WINDOWS = (2, 4, 8, 16)
POOL_HALO = 16
CONV_WIDTH = 3
CONV_HALO = 8
GMLP_BLOCK = 128
GMLP_GROUPS = 8
PEER_HEADS = 8
PEER_TOPK = 16
N_MIXERS = 3

LANES = 128
SUBLANES = 8
PACKED_ROWS = 16
VMEM_LIMIT_BYTES = 56 * 1024 * 1024

BF16 = jnp.bfloat16
F32 = jnp.float32
U32 = jnp.uint32


def _rms_modulate(x, g, shift, scale):
    r = lax.rsqrt(jnp.mean(x * x, axis=-1, keepdims=True) + EPS)
    return (x * r * g) * (1.0 + scale) + shift


_GELU_K = 0.7978845608028654


def _gelu(x):
    inner = x * (_GELU_K + (_GELU_K * 0.044715) * (x * x))
    hx = 0.5 * x
    return hx + hx * jnp.tanh(inner)


def _dot(a, b):
    return jnp.dot(a, b, preferred_element_type=F32)


def _dot_nt(a, b):
    return lax.dot_general(a, b, (((1,), (1,)), ((), ())), preferred_element_type=F32)


def _ada_kernel(c_ref, w_ref, b_ref, o_ref):
    c = c_ref[...]
    cond = (c * jax.nn.sigmoid(c)).astype(BF16)
    o_ref[...] = _dot(cond, w_ref[...].astype(BF16)) + b_ref[...]


def _ada_call(c, ada_w, ada_b):
    depth, d, n = ada_w.shape
    b = c.shape[0]
    tn = n // 4
    return pl.pallas_call(
        _ada_kernel,
        grid=(depth, n // tn),
        in_specs=[
            pl.BlockSpec((b, d), lambda l, j: (0, 0)),
            pl.BlockSpec((None, d, tn), lambda l, j: (l, 0, j)),
            pl.BlockSpec((None, 1, tn), lambda l, j: (l, 0, j)),
        ],
        out_specs=pl.BlockSpec((None, b, tn), lambda l, j: (l, 0, j)),
        out_shape=jax.ShapeDtypeStruct((depth, b, n), F32),
        compiler_params=pltpu.CompilerParams(
            dimension_semantics=("arbitrary", "arbitrary"), vmem_limit_bytes=VMEM_LIMIT_BYTES),
        name="adaln",
    )(c, ada_w, ada_b.reshape(depth, 1, n))


def _pool_kernel(x_ref, mod_ref, g_ref, win_ref, wgrp_ref, scale_ref, wout_ref, o_ref, ubuf):
    tb, d = x_ref.shape
    grp = d // len(POOL_WINDOWS)
    s = pl.program_id(1)

    @pl.when(s == 0)
    def _():
        ubuf[0:POOL_HALO, :] = jnp.zeros((POOL_HALO, d), F32)

    x = x_ref[...]
    h = _rms_modulate(x, g_ref[...], mod_ref[0], mod_ref[1]).astype(BF16)
    u = _dot(h, win_ref[...])
    ubuf[POOL_HALO:POOL_HALO + tb, :] = u

    pos = s * tb + lax.broadcasted_iota(jnp.int32, (tb, 1), 0)
    t1 = (pos + 1).astype(F32)
    zs = []
    for gi, w in enumerate(POOL_WINDOWS):
        lo = gi * grp
        acc = u[:, lo:lo + grp]
        for k in range(1, w):
            acc = acc + ubuf[POOL_HALO - k:POOL_HALO - k + tb, lo:lo + grp]
        mean = acc / jnp.minimum(t1, float(w))
        p = (mean - u[:, lo:lo + grp]).astype(BF16)
        zs.append(_dot(p, wgrp_ref[gi]))
    z = (jnp.concatenate(zs, axis=-1) * scale_ref[...]).astype(BF16)
    y = _dot(z, wout_ref[...])
    o_ref[...] = x + mod_ref[2] * y
    ubuf[0:POOL_HALO, :] = ubuf[tb:tb + POOL_HALO, :]


def _pool_call(x, mod, g, w_in, w_grp, scale, w_out, *, tb):
    b, s, d = x.shape
    ng, grp = w_grp.shape[0], w_grp.shape[1]
    return pl.pallas_call(
        _pool_kernel,
        grid=(b, s // tb),
        in_specs=[
            pl.BlockSpec((None, tb, d), lambda i, j: (i, j, 0)),
            pl.BlockSpec((None, 6, 1, d), lambda i, j: (i, 0, 0, 0)),
            pl.BlockSpec((1, d), lambda i, j: (0, 0)),
            pl.BlockSpec((d, d), lambda i, j: (0, 0)),
            pl.BlockSpec((ng, grp, grp), lambda i, j: (0, 0, 0)),
            pl.BlockSpec((1, d), lambda i, j: (0, 0)),
            pl.BlockSpec((d, d), lambda i, j: (0, 0)),
        ],
        out_specs=pl.BlockSpec((None, tb, d), lambda i, j: (i, j, 0)),
        out_shape=jax.ShapeDtypeStruct((b, s, d), F32),
        scratch_shapes=[pltpu.VMEM((tb + POOL_HALO, d), F32)],
        compiler_params=pltpu.CompilerParams(
            dimension_semantics=("arbitrary", "arbitrary"), vmem_limit_bytes=VMEM_LIMIT_BYTES),
        name="pool_mixer",
    )(x, mod, g, w_in, w_grp, scale, w_out)


def _conv_kernel(x_ref, mod_ref, g_ref, win_ref, cw_ref, wout_ref, o_ref, zbuf):
    tb, d = x_ref.shape
    s = pl.program_id(1)

    @pl.when(s == 0)
    def _():
        zbuf[0:CONV_HALO, :] = jnp.zeros((CONV_HALO, d), F32)

    x = x_ref[...]
    h = _rms_modulate(x, g_ref[...], mod_ref[0], mod_ref[1]).astype(BF16)
    proj = _dot(h, win_ref[...])
    bg, cg, xt = proj[:, 0:d], proj[:, d:2 * d], proj[:, 2 * d:3 * d]
    z = cg * xt
    zbuf[CONV_HALO:CONV_HALO + tb, :] = z
    zc = z * cw_ref[CONV_WIDTH - 1:CONV_WIDTH, :]
    for k in range(1, CONV_WIDTH):
        zc = zc + zbuf[CONV_HALO - k:CONV_HALO - k + tb, :] * cw_ref[CONV_WIDTH - 1 - k:CONV_WIDTH - k, :]
    y = _dot((bg * zc).astype(BF16), wout_ref[...])
    o_ref[...] = x + mod_ref[2] * y
    zbuf[0:CONV_HALO, :] = zbuf[tb:tb + CONV_HALO, :]


def _conv_call(x, mod, g, w_in, conv_w, w_out, *, tb):
    b, s, d = x.shape
    return pl.pallas_call(
        _conv_kernel,
        grid=(b, s // tb),
        in_specs=[
            pl.BlockSpec((None, tb, d), lambda i, j: (i, j, 0)),
            pl.BlockSpec((None, 6, 1, d), lambda i, j: (i, 0, 0, 0)),
            pl.BlockSpec((1, d), lambda i, j: (0, 0)),
            pl.BlockSpec((d, 3 * d), lambda i, j: (0, 0)),
            pl.BlockSpec((CONV_WIDTH, d), lambda i, j: (0, 0)),
            pl.BlockSpec((d, d), lambda i, j: (0, 0)),
        ],
        out_specs=pl.BlockSpec((None, tb, d), lambda i, j: (i, j, 0)),
        out_shape=jax.ShapeDtypeStruct((b, s, d), F32),
        scratch_shapes=[pltpu.VMEM((tb + CONV_HALO, d), F32)],
        compiler_params=pltpu.CompilerParams(
            dimension_semantics=("arbitrary", "arbitrary"), vmem_limit_bytes=VMEM_LIMIT_BYTES),
        name="conv_mixer",
    )(x, mod, g, w_in, conv_w, w_out)


def _gmlp_kernel(x_ref, mod_ref, g_ref, win_ref, lng_ref, lnb_ref, ws_ref, bs_ref, wout_ref, o_ref):
    tb, d = x_ref.shape
    half = wout_ref.shape[0]
    gdim = half // GMLP_GROUPS

    x = x_ref[...]
    h = _rms_modulate(x, g_ref[...], mod_ref[0], mod_ref[1]).astype(BF16)
    z = _gelu(_dot(h, win_ref[...]))
    u, v = z[:, 0:half], z[:, half:2 * half]
    mu = jnp.mean(v, axis=-1, keepdims=True)
    vc = v - mu
    var = jnp.mean(vc * vc, axis=-1, keepdims=True)
    v = ((vc * lax.rsqrt(var + EPS)) * lng_ref[...] + lnb_ref[...]).astype(BF16)

    row = lax.broadcasted_iota(jnp.int32, (GMLP_BLOCK, GMLP_BLOCK), 0) // CHUNK
    col = lax.broadcasted_iota(jnp.int32, (GMLP_BLOCK, GMLP_BLOCK), 1) // CHUNK
    causal = col <= row
    blocks = []
    for n in range(tb // GMLP_BLOCK):
        r0 = n * GMLP_BLOCK
        cols = []
        for gi in range(GMLP_GROUPS):
            ws = jnp.where(causal, ws_ref[gi], 0.0).astype(BF16)
            vs = _dot(ws, v[r0:r0 + GMLP_BLOCK, gi * gdim:(gi + 1) * gdim]) + bs_ref[gi]
            cols.append(vs)
        blocks.append(jnp.concatenate(cols, axis=-1))
    vs = jnp.concatenate(blocks, axis=0)
    y = _dot((u * vs).astype(BF16), wout_ref[...])
    o_ref[...] = x + mod_ref[2] * y


def _gmlp_call(x, mod, g, w_in, ln_g, ln_b, w_s, b_s, w_out, *, tb):
    b, s, d = x.shape
    half = w_out.shape[0]
    nblk = s // tb
    const = dict(pipeline_mode=pl.Buffered(1))
    return pl.pallas_call(
        _gmlp_kernel,
        grid=(b, nblk),
        in_specs=[
            pl.BlockSpec((None, tb, d), lambda i, j: (i, j, 0)),
            pl.BlockSpec((None, 6, 1, d), lambda i, j: (i, 0, 0, 0)),
            pl.BlockSpec((1, d), lambda i, j: (0, 0)),
            pl.BlockSpec((d, 2 * half), lambda i, j: (0, 0), **const),
            pl.BlockSpec((1, half), lambda i, j: (0, 0)),
            pl.BlockSpec((1, half), lambda i, j: (0, 0)),
            pl.BlockSpec((GMLP_GROUPS, GMLP_BLOCK, GMLP_BLOCK), lambda i, j: (0, 0, 0)),
            pl.BlockSpec((GMLP_GROUPS, GMLP_BLOCK, 1), lambda i, j: (0, 0, 0)),
            pl.BlockSpec((half, d), lambda i, j: (0, 0), **const),
        ],
        out_specs=pl.BlockSpec((None, tb, d), lambda i, j: (i, j, 0)),
        out_shape=jax.ShapeDtypeStruct((b, s, d), F32),
        compiler_params=pltpu.CompilerParams(
            dimension_semantics=("arbitrary", "arbitrary"), vmem_limit_bytes=VMEM_LIMIT_BYTES),
        name="gmlp_mixer",
    )(x, mod, g, w_in, ln_g, ln_b, w_s, b_s, w_out)


def _sorted_top(s, k, want_rank):
    rows = lax.broadcasted_iota(jnp.int32, (k, s.shape[1]), 0)
    out = jnp.zeros((k, s.shape[1]), F32)
    rank = jnp.full(s.shape, float(k), F32) if want_rank else None
    for i in range(k):
        m = jnp.max(s, axis=0, keepdims=True)
        eq = s == m
        out = jnp.where(rows == i, m, out)
        if want_rank:
            rank = jnp.where(eq, float(i), rank)
        s = jnp.where(eq, -jnp.inf, s)
    return out, rank


def _dup_bf16_bits(x):
    hi = pltpu.bitcast(x, U32) & jnp.uint32(0xFFFF0000)
    return hi | (hi >> 16)


def _route_tile(s0, s1):
    k = PEER_TOPK
    a, _ = _sorted_top(s0, k, False)
    b, rank1 = _sorted_top(s1, k, True)
    parts = [a[0:1] + b]
    for r in range(1, 8):
        parts.append(a[r:r + 1] + b[0:8])
    parts.append(a[8:16] + b[0:1])
    cand = jnp.concatenate(parts, axis=0)
    top = cand[0:1]
    work = cand
    tau = top
    for _ in range(k):
        tau = jnp.max(work, axis=0, keepdims=True)
        work = jnp.where(work == tau, -jnp.inf, work)
    z = jnp.sum(jnp.where(cand >= tau, jnp.exp(cand - top), 0.0), axis=0, keepdims=True)

    cnt_r = jnp.zeros(a.shape, F32)
    for c in range(k):
        cnt_r = cnt_r + jnp.where(a + b[c:c + 1] >= tau, 1.0, 0.0)
    cnt = jnp.zeros(s0.shape, F32)
    for r in range(k):
        cnt = jnp.where(s0 == a[r:r + 1], cnt_r[r:r + 1], cnt)

    a0 = (jnp.exp(s0 - a[0:1]) / z).astype(BF16).astype(F32)
    e1 = jnp.exp(s1 - b[0:1]).astype(BF16)
    return rank1.astype(BF16), e1, _dup_bf16_bits(cnt), _dup_bf16_bits(a0)


def _peer_kernel(x_ref, mod_ref, g_ref, fg_ref, wq_ref, keys_ref, u_ref, vt_ref, o_ref,
                 ht_scr, q_scr, col_scr, cnt_scr, a0_scr, y0_scr, y1_scr, p0_scr, p1_scr,
                 acc_scr,
                 *, route_lanes, mxu_rows, final_norm):
    tb, d = x_ref.shape
    ec = u_ref.shape[0]
    heads, _, nk, dh = keys_ref.shape
    c = pl.program_id(1)
    nc = pl.num_programs(1) - 2
    rows_per_step = ec // nk
    n_lt = tb // LANES
    n_rt = tb // route_lanes
    n_jg = nk // PACKED_ROWS

    y_bufs = (y0_scr, y1_scr)
    p_bufs = (p0_scr, p1_scr)

    def stage_a(slot):
        part = {}

        def item(m, k):
            rows = slice(m * mxu_rows, (m + 1) * mxu_rows)
            ks = slice(k * mxu_rows, (k + 1) * mxu_rows)
            prod = _dot(u_ref[rows, ks], ht_scr[ks, :])
            part[m] = prod if k == 0 else part[m] + prod
            if k == d // mxu_rows - 1:
                y_bufs[slot][rows, :] = part.pop(m)

        return [functools.partial(item, m, k) for m in range(ec // mxu_rows)
                for k in range(d // mxu_rows)]

    def stage_b(slot):
        cb = c - 1
        y_scr, p_scr = y_bufs[slot], p_bufs[slot]

        def item(ii, lt):
            grp = cb * (rows_per_step // SUBLANES) + ii // SUBLANES
            sl = ii % SUBLANES
            lanes = slice(lt * LANES, (lt + 1) * LANES)
            cnt_rows, a0_rows = [], []
            for hd in range(heads):
                cw = jnp.broadcast_to(cnt_scr[hd, grp, sl:sl + 1, lanes], (SUBLANES, LANES))
                aw = jnp.broadcast_to(a0_scr[hd, grp, sl:sl + 1, lanes], (SUBLANES, LANES))
                cnt_rows.append(pltpu.bitcast(cw, BF16))
                a0_rows.append(pltpu.bitcast(aw, BF16))
            for jg in range(n_jg):
                w = jnp.zeros((PACKED_ROWS, LANES), BF16)
                base = (lt * n_jg + jg) * 2 * PACKED_ROWS
                for hd in range(heads):
                    rank = col_scr[hd, base:base + PACKED_ROWS, :]
                    e1 = col_scr[hd, base + PACKED_ROWS:base + 2 * PACKED_ROWS, :]
                    w = w + jnp.where(rank < cnt_rows[hd], a0_rows[hd] * e1, 0.0)
                r0 = ii * nk + jg * PACKED_ROWS
                act = _gelu(y_scr[r0:r0 + PACKED_ROWS, lanes]).astype(BF16)
                p_scr[r0:r0 + PACKED_ROWS, lanes] = w * act

        return [functools.partial(item, ii, lt) for ii in range(rows_per_step)
                for lt in range(n_lt)]

    def stage_c(slot):
        part = {}

        def item(m, k):
            rows = slice(m * mxu_rows, (m + 1) * mxu_rows)
            ks = slice(k * mxu_rows, (k + 1) * mxu_rows)
            prod = _dot(vt_ref[rows, ks], p_bufs[slot][ks, :])
            part[m] = prod if k == 0 else part[m] + prod
            if k == ec // mxu_rows - 1:
                acc_scr[rows, :] += part.pop(m)

        return [functools.partial(item, m, k) for m in range(d // mxu_rows)
                for k in range(ec // mxu_rows)]

    def run(mxu_items, valu_items):
        n_m, n_v = len(mxu_items), len(valu_items)
        if n_m == 0 or n_v == 0:
            for f in mxu_items + valu_items:
                f()
            return
        done_v = 0
        for k, f in enumerate(mxu_items):
            f()
            upto = ((k + 1) * n_v) // n_m
            for g in valu_items[done_v:upto]:
                g()
            done_v = upto

    def interleave(a, b):
        out = []
        for k in range(max(len(a), len(b))):
            out += a[k:k + 1] + b[k:k + 1]
        return out

    @pl.when(c == 0)
    def _first():
        x = x_ref[...]
        h = _rms_modulate(x, g_ref[...], mod_ref[3], mod_ref[4])
        ht_scr[...] = h.T.astype(BF16)
        q_scr[...] = _dot(wq_ref[...], ht_scr[...]).astype(BF16)
        acc_scr[...] = jnp.zeros_like(acc_scr)

        def route(it, carry):
            hd = it // n_rt
            lanes = pl.ds(pl.multiple_of((it % n_rt) * route_lanes, route_lanes), route_lanes)
            q0 = q_scr[pl.ds(pl.multiple_of(hd * 2 * dh, dh), dh), lanes]
            q1 = q_scr[pl.ds(pl.multiple_of(hd * 2 * dh + dh, dh), dh), lanes]
            s0 = _dot(keys_ref[hd, 0], q0)
            s1 = _dot(keys_ref[hd, 1], q1)
            rank1, e1, cnt, a0 = _route_tile(s0, s1)
            for ls in range(route_lanes // LANES):
                tiles = []
                for jg in range(n_jg):
                    blk = (slice(jg * PACKED_ROWS, (jg + 1) * PACKED_ROWS),
                           slice(ls * LANES, (ls + 1) * LANES))
                    tiles += [rank1[blk], e1[blk]]
                lt = (it % n_rt) * (route_lanes // LANES) + ls
                start = pl.multiple_of(lt * (n_jg * 2 * PACKED_ROWS), n_jg * 2 * PACKED_ROWS)
                col_scr[hd, pl.ds(start, n_jg * 2 * PACKED_ROWS), :] = jnp.concatenate(tiles, axis=0)
            cnt_scr[hd, :, :, lanes] = cnt.reshape(nk // SUBLANES, SUBLANES, route_lanes)
            a0_scr[hd, :, :, lanes] = a0.reshape(nk // SUBLANES, SUBLANES, route_lanes)
            return carry

        lax.fori_loop(0, heads * n_rt, route, 0)
        run(stage_a(0), [])

    @pl.when(c == 1)
    def _fill():
        run(stage_a(1), stage_b(0))

    steady = jnp.logical_and(c >= 2, c < nc)

    @pl.when(jnp.logical_and(steady, c % 2 == 0))
    def _steady_even():
        run(interleave(stage_a(0), stage_c(0)), stage_b(1))

    @pl.when(jnp.logical_and(steady, c % 2 == 1))
    def _steady_odd():
        run(interleave(stage_a(1), stage_c(1)), stage_b(0))

    @pl.when(c == nc)
    def _drain():
        run(stage_c(0), stage_b(1))

    @pl.when(c == nc + 1)
    def _last():
        run(stage_c(1), [])
        x = x_ref[...]
        out = x + mod_ref[5] * acc_scr[...].T
        if final_norm:
            r = lax.rsqrt(jnp.mean(out * out, axis=-1, keepdims=True) + EPS)
            out = out * r * fg_ref[...]
        o_ref[...] = out


def _peer_call(x2, mod, g, final_g, wq_t, keys, u_tab, vt_tab, *, layer, seq, tb, ec, route_lanes, mxu_rows,
               final_norm):
    t, d = x2.shape
    n_exp = u_tab.shape[1]
    heads, _, nk, dh = keys.shape[1:]
    tiles_per_seq = seq // tb
    nc = n_exp // ec
    assert ec % (SUBLANES * nk) == 0 and nk % PACKED_ROWS == 0 and nc >= 4 and nc % 2 == 0
    assert seq % tb == 0 and tb % route_lanes == 0 and route_lanes % LANES == 0
    kern = functools.partial(_peer_kernel, route_lanes=route_lanes, mxu_rows=mxu_rows,
                             final_norm=final_norm)
    return pl.pallas_call(
        kern,
        grid=(t // tb, nc + 2),
        in_specs=[
            pl.BlockSpec((tb, d), lambda i, c: (i, 0)),
            pl.BlockSpec((None, 6, 1, d), lambda i, c: (i // tiles_per_seq, 0, 0, 0)),
            pl.BlockSpec((1, d), lambda i, c: (0, 0)),
            pl.BlockSpec((1, d), lambda i, c: (0, 0)),
            pl.BlockSpec((None, heads * 2 * dh, d), lambda i, c: (layer, 0, 0)),
            pl.BlockSpec((None, heads, 2, nk, dh), lambda i, c: (layer, 0, 0, 0, 0)),
            pl.BlockSpec((None, ec, d), lambda i, c: (layer, jnp.minimum(c, nc - 1), 0)),
            pl.BlockSpec((None, d, ec), lambda i, c: (layer, 0, jnp.clip(c - 2, 0, nc - 1))),
        ],
        out_specs=pl.BlockSpec((tb, d), lambda i, c: (i, 0)),
        out_shape=jax.ShapeDtypeStruct((t, d), F32),
        scratch_shapes=[
            pltpu.VMEM((d, tb), BF16),
            pltpu.VMEM((heads * 2 * dh, tb), BF16),
            pltpu.VMEM((heads, (tb // LANES) * (nk // PACKED_ROWS) * 2 * PACKED_ROWS + PACKED_ROWS,
                        LANES), BF16),
            pltpu.VMEM((heads, nk // SUBLANES, SUBLANES, tb), U32),
            pltpu.VMEM((heads, nk // SUBLANES, SUBLANES, tb), U32),
            pltpu.VMEM((ec, tb), F32),
            pltpu.VMEM((ec, tb), F32),
            pltpu.VMEM((ec, tb), BF16),
            pltpu.VMEM((ec, tb), BF16),
            pltpu.VMEM((d, tb), F32),
        ],
        compiler_params=pltpu.CompilerParams(
            dimension_semantics=("arbitrary", "arbitrary"), vmem_limit_bytes=VMEM_LIMIT_BYTES),
        name="peer",
    )(x2, mod, g, final_g, wq_t, keys, u_tab, vt_tab)


def _tiles(seq, n_exp):
    tb_mix = min(seq, 512)
    tb_gmlp = min(seq, 256)
    tb_peer = min(seq, 512)
    ec = min(n_exp // 4, 1024)
    route_lanes = min(tb_peer, 256)
    mxu_rows = 256
    return tb_mix, tb_gmlp, tb_peer, ec, route_lanes, mxu_rows


def kernel(x, c, ada_w, ada_b, norm_g, pool_w_in, pool_w_grp, pool_scale, pool_w_out, conv_w_in, conv_w, conv_w_out, gmlp_w_in, gmlp_ln_g, gmlp_ln_b, gmlp_w_s, gmlp_b_s, gmlp_w_out, peer_w_q, peer_keys, peer_u, peer_v, final_g):
    b, seq, d = x.shape
    depth = ada_w.shape[0]
    n_exp = peer_u.shape[1]
    tb_mix, tb_gmlp, tb_peer, ec, route_lanes, mxu_rows = _tiles(seq, n_exp)

    mod = _ada_call(c, ada_w, ada_b).reshape(depth, b, 6, 1, d)

    wq_t = jnp.swapaxes(peer_w_q, 1, 2).astype(BF16)
    keys = peer_keys.astype(BF16)
    u_tab = peer_u.astype(BF16)
    vt_tab = jnp.swapaxes(peer_v, 1, 2).astype(BF16)
    fg = final_g.reshape(1, d)

    for i in range(depth):
        kind, j = i % N_MIXERS, i // N_MIXERS
        g1 = norm_g[i, 0].reshape(1, d)
        if kind == 0:
            x = _pool_call(x, mod[i], g1, pool_w_in[j].astype(BF16), pool_w_grp[j].astype(BF16),
                           pool_scale[j].reshape(1, d), pool_w_out[j].astype(BF16), tb=tb_mix)
        elif kind == 1:
            x = _conv_call(x, mod[i], g1, conv_w_in[j].astype(BF16), conv_w[j],
                           conv_w_out[j].astype(BF16), tb=tb_mix)
        else:
            half = gmlp_w_out.shape[1]
            x = _gmlp_call(x, mod[i], g1, gmlp_w_in[j].astype(BF16), gmlp_ln_g[j].reshape(1, half),
                           gmlp_ln_b[j].reshape(1, half), gmlp_w_s[j],
                           gmlp_b_s[j].reshape(GMLP_GROUPS, GMLP_BLOCK, 1),
                           gmlp_w_out[j].astype(BF16), tb=tb_gmlp)
        x = _peer_call(x.reshape(b * seq, d), mod[i], norm_g[i, 1].reshape(1, d), fg, wq_t, keys,
                       u_tab, vt_tab, layer=i, seq=seq, tb=tb_peer, ec=ec, route_lanes=route_lanes, mxu_rows=mxu_rows,
                       final_norm=(i == depth - 1)).reshape(b, seq, d)
    return x
```

```python
import functools

import jax
import jax.numpy as jnp
from jax import lax
from jax.experimental import pallas as pl
from jax.experimental.pallas import tpu as pltpu

EPS = 1e-6
CHUNK = 64
POOL_WINDOWS = (2, 4, 8, 16)
POOL_HALO = 16
CONV_WIDTH = 3
CONV_HALO = 8
GMLP_BLOCK = 128
GMLP_GROUPS = 8
PEER_HEADS = 8
PEER_TOPK = 16
N_MIXERS = 3

LANES = 128
SUBLANES = 8
PACKED_ROWS = 16
VMEM_LIMIT_BYTES = 56 * 1024 * 1024

BF16 = jnp.bfloat16
F32 = jnp.float32


def _rms_modulate(x, g, shift, scale):
    r = lax.rsqrt(jnp.mean(x * x, axis=-1, keepdims=True) + EPS)
    return (x * r * g) * (1.0 + scale) + shift


_GELU_K = 0.7978845608028654


def _gelu(x):
    inner = x * (_GELU_K + (_GELU_K * 0.044715) * (x * x))
    hx = 0.5 * x
    return hx + hx * jnp.tanh(inner)


def _dot(a, b):
    return jnp.dot(a, b, preferred_element_type=F32)


def _dot_nt(a, b):
    return lax.dot_general(a, b, (((1,), (1,)), ((), ())), preferred_element_type=F32)


def _ada_kernel(c_ref, w_ref, b_ref, o_ref):
    c = c_ref[...]
    cond = (c * jax.nn.sigmoid(c)).astype(BF16)
    o_ref[...] = _dot(cond, w_ref[...].astype(BF16)) + b_ref[...]


def _ada_call(c, ada_w, ada_b):
    depth, d, n = ada_w.shape
    b = c.shape[0]
    tn = n // 4
    return pl.pallas_call(
        _ada_kernel,
        grid=(depth, n // tn),
        in_specs=[
            pl.BlockSpec((b, d), lambda l, j: (0, 0)),
            pl.BlockSpec((None, d, tn), lambda l, j: (l, 0, j)),
            pl.BlockSpec((None, 1, tn), lambda l, j: (l, 0, j)),
        ],
        out_specs=pl.BlockSpec((None, b, tn), lambda l, j: (l, 0, j)),
        out_shape=jax.ShapeDtypeStruct((depth, b, n), F32),
        compiler_params=pltpu.CompilerParams(
            dimension_semantics=("arbitrary", "arbitrary"), vmem_limit_bytes=VMEM_LIMIT_BYTES),
        name="adaln",
    )(c, ada_w, ada_b.reshape(depth, 1, n))


def _pool_kernel(x_ref, mod_ref, g_ref, win_ref, wgrp_ref, scale_ref, wout_ref, o_ref, ubuf):
    tb, d = x_ref.shape
    grp = d // len(POOL_WINDOWS)
    s = pl.program_id(1)

    @pl.when(s == 0)
    def _():
        ubuf[0:POOL_HALO, :] = jnp.zeros((POOL_HALO, d), F32)

    x = x_ref[...]
    h = _rms_modulate(x, g_ref[...], mod_ref[0], mod_ref[1]).astype(BF16)
    u = _dot(h, win_ref[...])
    ubuf[POOL_HALO:POOL_HALO + tb, :] = u

    pos = s * tb + lax.broadcasted_iota(jnp.int32, (tb, 1), 0)
    t1 = (pos + 1).astype(F32)
    zs = []
    for gi, w in enumerate(POOL_WINDOWS):
        lo = gi * grp
        acc = u[:, lo:lo + grp]
        for k in range(1, w):
            acc = acc + ubuf[POOL_HALO - k:POOL_HALO - k + tb, lo:lo + grp]
        mean = acc / jnp.minimum(t1, float(w))
        p = (mean - u[:, lo:lo + grp]).astype(BF16)
        zs.append(_dot(p, wgrp_ref[gi]))
    z = (jnp.concatenate(zs, axis=-1) * scale_ref[...]).astype(BF16)
    y = _dot(z, wout_ref[...])
    o_ref[...] = x + mod_ref[2] * y
    ubuf[0:POOL_HALO, :] = ubuf[tb:tb + POOL_HALO, :]


def _pool_call(x, mod, g, w_in, w_grp, scale, w_out, *, tb):
    b, s, d = x.shape
    ng, grp = w_grp.shape[0], w_grp.shape[1]
    return pl.pallas_call(
        _pool_kernel,
        grid=(b, s // tb),
        in_specs=[
            pl.BlockSpec((None, tb, d), lambda i, j: (i, j, 0)),
            pl.BlockSpec((None, 6, 1, d), lambda i, j: (i, 0, 0, 0)),
            pl.BlockSpec((1, d), lambda i, j: (0, 0)),
            pl.BlockSpec((d, d), lambda i, j: (0, 0)),
            pl.BlockSpec((ng, grp, grp), lambda i, j: (0, 0, 0)),
            pl.BlockSpec((1, d), lambda i, j: (0, 0)),
            pl.BlockSpec((d, d), lambda i, j: (0, 0)),
        ],
        out_specs=pl.BlockSpec((None, tb, d), lambda i, j: (i, j, 0)),
        out_shape=jax.ShapeDtypeStruct((b, s, d), F32),
        scratch_shapes=[pltpu.VMEM((tb + POOL_HALO, d), F32)],
        compiler_params=pltpu.CompilerParams(
            dimension_semantics=("arbitrary", "arbitrary"), vmem_limit_bytes=VMEM_LIMIT_BYTES),
        name="pool_mixer",
    )(x, mod, g, w_in, w_grp, scale, w_out)


def _conv_kernel(x_ref, mod_ref, g_ref, win_ref, cw_ref, wout_ref, o_ref, zbuf):
    tb, d = x_ref.shape
    s = pl.program_id(1)

    @pl.when(s == 0)
    def _():
        zbuf[0:CONV_HALO, :] = jnp.zeros((CONV_HALO, d), F32)

    x = x_ref[...]
    h = _rms_modulate(x, g_ref[...], mod_ref[0], mod_ref[1]).astype(BF16)
    proj = _dot(h, win_ref[...])
    bg, cg, xt = proj[:, 0:d], proj[:, d:2 * d], proj[:, 2 * d:3 * d]
    z = cg * xt
    zbuf[CONV_HALO:CONV_HALO + tb, :] = z
    zc = z * cw_ref[CONV_WIDTH - 1:CONV_WIDTH, :]
    for k in range(1, CONV_WIDTH):
        zc = zc + zbuf[CONV_HALO - k:CONV_HALO - k + tb, :] * cw_ref[CONV_WIDTH - 1 - k:CONV_WIDTH - k, :]
    y = _dot((bg * zc).astype(BF16), wout_ref[...])
    o_ref[...] = x + mod_ref[2] * y
    zbuf[0:CONV_HALO, :] = zbuf[tb:tb + CONV_HALO, :]


def _conv_call(x, mod, g, w_in, conv_w, w_out, *, tb):
    b, s, d = x.shape
    return pl.pallas_call(
        _conv_kernel,
        grid=(b, s // tb),
        in_specs=[
            pl.BlockSpec((None, tb, d), lambda i, j: (i, j, 0)),
            pl.BlockSpec((None, 6, 1, d), lambda i, j: (i, 0, 0, 0)),
            pl.BlockSpec((1, d), lambda i, j: (0, 0)),
            pl.BlockSpec((d, 3 * d), lambda i, j: (0, 0)),
            pl.BlockSpec((CONV_WIDTH, d), lambda i, j: (0, 0)),
            pl.BlockSpec((d, d), lambda i, j: (0, 0)),
        ],
        out_specs=pl.BlockSpec((None, tb, d), lambda i, j: (i, j, 0)),
        out_shape=jax.ShapeDtypeStruct((b, s, d), F32),
        scratch_shapes=[pltpu.VMEM((tb + CONV_HALO, d), F32)],
        compiler_params=pltpu.CompilerParams(
            dimension_semantics=("arbitrary", "arbitrary"), vmem_limit_bytes=VMEM_LIMIT_BYTES),
        name="conv_mixer",
    )(x, mod, g, w_in, conv_w, w_out)


def _gmlp_kernel(x_ref, mod_ref, g_ref, win_ref, lng_ref, lnb_ref, ws_ref, bs_ref, wout_ref, o_ref):
    tb, d = x_ref.shape
    half = wout_ref.shape[0]
    gdim = half // GMLP_GROUPS

    x = x_ref[...]
    h = _rms_modulate(x, g_ref[...], mod_ref[0], mod_ref[1]).astype(BF16)
    z = _gelu(_dot(h, win_ref[...]))
    u, v = z[:, 0:half], z[:, half:2 * half]
    mu = jnp.mean(v, axis=-1, keepdims=True)
    vc = v - mu
    var = jnp.mean(vc * vc, axis=-1, keepdims=True)
    v = ((vc * lax.rsqrt(var + EPS)) * lng_ref[...] + lnb_ref[...]).astype(BF16)

    row = lax.broadcasted_iota(jnp.int32, (GMLP_BLOCK, GMLP_BLOCK), 0) // CHUNK
    col = lax.broadcasted_iota(jnp.int32, (GMLP_BLOCK, GMLP_BLOCK), 1) // CHUNK
    causal = col <= row
    blocks = []
    for n in range(tb // GMLP_BLOCK):
        r0 = n * GMLP_BLOCK
        cols = []
        for gi in range(GMLP_GROUPS):
            ws = jnp.where(causal, ws_ref[gi], 0.0).astype(BF16)
            vs = _dot(ws, v[r0:r0 + GMLP_BLOCK, gi * gdim:(gi + 1) * gdim]) + bs_ref[gi]
            cols.append(vs)
        blocks.append(jnp.concatenate(cols, axis=-1))
    vs = jnp.concatenate(blocks, axis=0)
    y = _dot((u * vs).astype(BF16), wout_ref[...])
    o_ref[...] = x + mod_ref[2] * y


def _gmlp_call(x, mod, g, w_in, ln_g, ln_b, w_s, b_s, w_out, *, tb):
    b, s, d = x.shape
    half = w_out.shape[0]
    nblk = s // tb
    const = dict(pipeline_mode=pl.Buffered(1))
    return pl.pallas_call(
        _gmlp_kernel,
        grid=(b, nblk),
        in_specs=[
            pl.BlockSpec((None, tb, d), lambda i, j: (i, j, 0)),
            pl.BlockSpec((None, 6, 1, d), lambda i, j: (i, 0, 0, 0)),
            pl.BlockSpec((1, d), lambda i, j: (0, 0)),
            pl.BlockSpec((d, 2 * half), lambda i, j: (0, 0), **const),
            pl.BlockSpec((1, half), lambda i, j: (0, 0)),
            pl.BlockSpec((1, half), lambda i, j: (0, 0)),
            pl.BlockSpec((GMLP_GROUPS, GMLP_BLOCK, GMLP_BLOCK), lambda i, j: (0, 0, 0)),
            pl.BlockSpec((GMLP_GROUPS, GMLP_BLOCK, 1), lambda i, j: (0, 0, 0)),
            pl.BlockSpec((half, d), lambda i, j: (0, 0), **const),
        ],
        out_specs=pl.BlockSpec((None, tb, d), lambda i, j: (i, j, 0)),
        out_shape=jax.ShapeDtypeStruct((b, s, d), F32),
        compiler_params=pltpu.CompilerParams(
            dimension_semantics=("arbitrary", "arbitrary"), vmem_limit_bytes=VMEM_LIMIT_BYTES),
        name="gmlp_mixer",
    )(x, mod, g, w_in, ln_g, ln_b, w_s, b_s, w_out)


def _sorted_top(s, k, want_rank):
    rows = lax.broadcasted_iota(jnp.int32, (k, s.shape[1]), 0)
    out = jnp.zeros((k, s.shape[1]), F32)
    rank = jnp.full(s.shape, float(k), F32) if want_rank else None
    for i in range(k):
        m = jnp.max(s, axis=0, keepdims=True)
        eq = s == m
        out = jnp.where(rows == i, m, out)
        if want_rank:
            rank = jnp.where(eq, float(i), rank)
        s = jnp.where(eq, -jnp.inf, s)
    return out, rank


def _route_tile(s0, s1):
    k = PEER_TOPK
    a, _ = _sorted_top(s0, k, False)
    b, rank1 = _sorted_top(s1, k, True)
    parts = [a[0:1] + b]
    for r in range(1, 8):
        parts.append(a[r:r + 1] + b[0:8])
    parts.append(a[8:16] + b[0:1])
    cand = jnp.concatenate(parts, axis=0)
    top = cand[0:1]
    work = cand
    tau = top
    for _ in range(k):
        tau = jnp.max(work, axis=0, keepdims=True)
        work = jnp.where(work == tau, -jnp.inf, work)
    z = jnp.sum(jnp.where(cand >= tau, jnp.exp(cand - top), 0.0), axis=0, keepdims=True)

    cnt_r = jnp.zeros(a.shape, F32)
    for c in range(k):
        cnt_r = cnt_r + jnp.where(a + b[c:c + 1] >= tau, 1.0, 0.0)
    cnt = jnp.zeros(s0.shape, F32)
    for r in range(k):
        cnt = jnp.where(s0 == a[r:r + 1], cnt_r[r:r + 1], cnt)

    a0 = jnp.exp(s0 - a[0:1]) / z
    e1 = jnp.exp(s1 - b[0:1])
    return rank1, e1, cnt, a0


def _peer_kernel(x_ref, mod_ref, g_ref, fg_ref, wq_ref, keys_ref, u_ref, vt_ref, o_ref,
                 ht_scr, q_scr, col_scr, cnt_scr, a0_scr, y0_scr, y1_scr, p0_scr, p1_scr,
                 acc_scr,
                 *, route_lanes, mxu_rows, final_norm):
    tb, d = x_ref.shape
    ec = u_ref.shape[0]
    heads, _, nk, dh = keys_ref.shape
    c = pl.program_id(1)
    nc = pl.num_programs(1) - 2
    rows_per_step = ec // nk
    n_lt = tb // LANES
    n_rt = tb // route_lanes
    n_jg = nk // SUBLANES

    y_bufs = (y0_scr, y1_scr)
    p_bufs = (p0_scr, p1_scr)

    def stage_a(slot):
        part = {}

        def item(m, k):
            rows = slice(m * mxu_rows, (m + 1) * mxu_rows)
            ks = slice(k * mxu_rows, (k + 1) * mxu_rows)
            prod = _dot(u_ref[rows, ks], ht_scr[ks, :])
            part[m] = prod if k == 0 else part[m] + prod
            if k == d // mxu_rows - 1:
                y_bufs[slot][rows, :] = part.pop(m)

        return [functools.partial(item, m, k) for m in range(ec // mxu_rows)
                for k in range(d // mxu_rows)]

    def stage_b(slot):
        cb = c - 1
        y_scr, p_scr = y_bufs[slot], p_bufs[slot]

        def item(ii, lt):
            grp = cb * (rows_per_step // SUBLANES) + ii // SUBLANES
            sl = ii % SUBLANES
            lanes = slice(lt * LANES, (lt + 1) * LANES)
            cnt_rows, a0_rows = [], []
            for hd in range(heads):
                cnt_rows.append(jnp.broadcast_to(cnt_scr[hd, grp, sl:sl + 1, lanes],
                                                 (SUBLANES, LANES)))
                a0_rows.append(jnp.broadcast_to(a0_scr[hd, grp, sl:sl + 1, lanes],
                                                (SUBLANES, LANES)))
            for jp in range(n_jg // 2):
                gated = []
                for jg in (2 * jp, 2 * jp + 1):
                    w = jnp.zeros((SUBLANES, LANES), F32)
                    base = (lt * n_jg + jg) * 2 * SUBLANES
                    for hd in range(heads):
                        rank = col_scr[hd, base:base + SUBLANES, :]
                        e1 = col_scr[hd, base + SUBLANES:base + 2 * SUBLANES, :]
                        w = jnp.where(rank < cnt_rows[hd], w + a0_rows[hd] * e1, w)
                    r0 = ii * nk + jg * SUBLANES
                    gated.append(w * _gelu(y_scr[r0:r0 + SUBLANES, lanes]))
                r0 = ii * nk + jp * PACKED_ROWS
                p_scr[r0:r0 + PACKED_ROWS, lanes] = jnp.concatenate(gated, axis=0).astype(BF16)

        return [functools.partial(item, ii, lt) for ii in range(rows_per_step)
                for lt in range(n_lt)]

    def stage_c(slot):
        part = {}

        def item(m, k):
            rows = slice(m * mxu_rows, (m + 1) * mxu_rows)
            ks = slice(k * mxu_rows, (k + 1) * mxu_rows)
            prod = _dot(vt_ref[rows, ks], p_bufs[slot][ks, :])
            part[m] = prod if k == 0 else part[m] + prod
            if k == ec // mxu_rows - 1:
                acc_scr[rows, :] += part.pop(m)

        return [functools.partial(item, m, k) for m in range(d // mxu_rows)
                for k in range(ec // mxu_rows)]

    def run(mxu_items, valu_items):
        n_m, n_v = len(mxu_items), len(valu_items)
        if n_m == 0 or n_v == 0:
            for f in mxu_items + valu_items:
                f()
            return
        done_v = 0
        for k, f in enumerate(mxu_items):
            f()
            upto = ((k + 1) * n_v) // n_m
            for g in valu_items[done_v:upto]:
                g()
            done_v = upto

    def interleave(a, b):
        out = []
        for k in range(max(len(a), len(b))):
            out += a[k:k + 1] + b[k:k + 1]
        return out

    @pl.when(c == 0)
    def _first():
        x = x_ref[...]
        h = _rms_modulate(x, g_ref[...], mod_ref[3], mod_ref[4])
        ht_scr[...] = h.T.astype(BF16)
        q_scr[...] = _dot(wq_ref[...], ht_scr[...]).astype(BF16)
        acc_scr[...] = jnp.zeros_like(acc_scr)

        def route(it, carry):
            hd = it // n_rt
            lanes = pl.ds(pl.multiple_of((it % n_rt) * route_lanes, route_lanes), route_lanes)
            q0 = q_scr[pl.ds(pl.multiple_of(hd * 2 * dh, dh), dh), lanes]
            q1 = q_scr[pl.ds(pl.multiple_of(hd * 2 * dh + dh, dh), dh), lanes]
            s0 = _dot(keys_ref[hd, 0], q0)
            s1 = _dot(keys_ref[hd, 1], q1)
            rank1, e1, cnt, a0 = _route_tile(s0, s1)
            for ls in range(route_lanes // LANES):
                tiles = []
                for jg in range(n_jg):
                    blk = (slice(jg * SUBLANES, (jg + 1) * SUBLANES),
                           slice(ls * LANES, (ls + 1) * LANES))
                    tiles += [rank1[blk], e1[blk]]
                lt = (it % n_rt) * (route_lanes // LANES) + ls
                start = pl.multiple_of(lt * (n_jg * 2 * SUBLANES), n_jg * 2 * SUBLANES)
                col_scr[hd, pl.ds(start, n_jg * 2 * SUBLANES), :] = jnp.concatenate(tiles, axis=0)
            cnt_scr[hd, :, :, lanes] = cnt.reshape(nk // SUBLANES, SUBLANES, route_lanes)
            a0_scr[hd, :, :, lanes] = a0.reshape(nk // SUBLANES, SUBLANES, route_lanes)
            return carry

        lax.fori_loop(0, heads * n_rt, route, 0)
        run(stage_a(0), [])

    @pl.when(c == 1)
    def _fill():
        run(stage_a(1), stage_b(0))

    steady = jnp.logical_and(c >= 2, c < nc)

    @pl.when(jnp.logical_and(steady, c % 2 == 0))
    def _steady_even():
        run(interleave(stage_a(0), stage_c(0)), stage_b(1))

    @pl.when(jnp.logical_and(steady, c % 2 == 1))
    def _steady_odd():
        run(interleave(stage_a(1), stage_c(1)), stage_b(0))

    @pl.when(c == nc)
    def _drain():
        run(stage_c(0), stage_b(1))

    @pl.when(c == nc + 1)
    def _last():
        run(stage_c(1), [])
        x = x_ref[...]
        out = x + mod_ref[5] * acc_scr[...].T
        if final_norm:
            r = lax.rsqrt(jnp.mean(out * out, axis=-1, keepdims=True) + EPS)
            out = out * r * fg_ref[...]
        o_ref[...] = out


def _peer_call(x2, mod, g, final_g, wq_t, keys, u_tab, vt_tab, *, layer, seq, tb, ec, route_lanes, mxu_rows,
               final_norm):
    t, d = x2.shape
    n_exp = u_tab.shape[1]
    heads, _, nk, dh = keys.shape[1:]
    tiles_per_seq = seq // tb
    nc = n_exp // ec
    assert ec % (SUBLANES * nk) == 0 and nk % PACKED_ROWS == 0 and nc >= 4 and nc % 2 == 0
    assert seq % tb == 0 and tb % route_lanes == 0 and route_lanes % LANES == 0
    kern = functools.partial(_peer_kernel, route_lanes=route_lanes, mxu_rows=mxu_rows,
                             final_norm=final_norm)
    return pl.pallas_call(
        kern,
        grid=(t // tb, nc + 2),
        in_specs=[
            pl.BlockSpec((tb, d), lambda i, c: (i, 0)),
            pl.BlockSpec((None, 6, 1, d), lambda i, c: (i // tiles_per_seq, 0, 0, 0)),
            pl.BlockSpec((1, d), lambda i, c: (0, 0)),
            pl.BlockSpec((1, d), lambda i, c: (0, 0)),
            pl.BlockSpec((None, heads * 2 * dh, d), lambda i, c: (layer, 0, 0)),
            pl.BlockSpec((None, heads, 2, nk, dh), lambda i, c: (layer, 0, 0, 0, 0)),
            pl.BlockSpec((None, ec, d), lambda i, c: (layer, jnp.minimum(c, nc - 1), 0)),
            pl.BlockSpec((None, d, ec), lambda i, c: (layer, 0, jnp.clip(c - 2, 0, nc - 1))),
        ],
        out_specs=pl.BlockSpec((tb, d), lambda i, c: (i, 0)),
        out_shape=jax.ShapeDtypeStruct((t, d), F32),
        scratch_shapes=[
            pltpu.VMEM((d, tb), BF16),
            pltpu.VMEM((heads * 2 * dh, tb), BF16),
            pltpu.VMEM((heads, (tb // LANES) * (nk // SUBLANES) * 2 * SUBLANES + SUBLANES, LANES),
                       F32),
            pltpu.VMEM((heads, nk // SUBLANES, SUBLANES, tb), F32),
            pltpu.VMEM((heads, nk // SUBLANES, SUBLANES, tb), F32),
            pltpu.VMEM((ec, tb), F32),
            pltpu.VMEM((ec, tb), F32),
            pltpu.VMEM((ec, tb), BF16),
            pltpu.VMEM((ec, tb), BF16),
            pltpu.VMEM((d, tb), F32),
        ],
        compiler_params=pltpu.CompilerParams(
            dimension_semantics=("arbitrary", "arbitrary"), vmem_limit_bytes=VMEM_LIMIT_BYTES),
        name="peer",
    )(x2, mod, g, final_g, wq_t, keys, u_tab, vt_tab)


def _tiles(seq, n_exp):
    tb_mix = min(seq, 512)
    tb_gmlp = min(seq, 256)
    tb_peer = min(seq, 512)
    ec = min(n_exp // 4, 1024)
    route_lanes = min(tb_peer, 256)
    mxu_rows = 256
    return tb_mix, tb_gmlp, tb_peer, ec, route_lanes, mxu_rows


def kernel(x, c, ada_w, ada_b, norm_g, pool_w_in, pool_w_grp, pool_scale, pool_w_out, conv_w_in, conv_w, conv_w_out, gmlp_w_in, gmlp_ln_g, gmlp_ln_b, gmlp_w_s, gmlp_b_s, gmlp_w_out, peer_w_q, peer_keys, peer_u, peer_v, final_g):
    b, seq, d = x.shape
    depth = ada_w.shape[0]
    n_exp = peer_u.shape[1]
    tb_mix, tb_gmlp, tb_peer, ec, route_lanes, mxu_rows = _tiles(seq, n_exp)

    mod = _ada_call(c, ada_w, ada_b).reshape(depth, b, 6, 1, d)

    wq_t = jnp.swapaxes(peer_w_q, 1, 2).astype(BF16)
    keys = peer_keys.astype(BF16)
    u_tab = peer_u.astype(BF16)
    vt_tab = jnp.swapaxes(peer_v, 1, 2).astype(BF16)
    fg = final_g.reshape(1, d)

    for i in range(depth):
        kind, j = i % N_MIXERS, i // N_MIXERS
        g1 = norm_g[i, 0].reshape(1, d)
        if kind == 0:
            x = _pool_call(x, mod[i], g1, pool_w_in[j].astype(BF16), pool_w_grp[j].astype(BF16),
                           pool_scale[j].reshape(1, d), pool_w_out[j].astype(BF16), tb=tb_mix)
        elif kind == 1:
            x = _conv_call(x, mod[i], g1, conv_w_in[j].astype(BF16), conv_w[j],
                           conv_w_out[j].astype(BF16), tb=tb_mix)
        else:
            half = gmlp_w_out.shape[1]
            x = _gmlp_call(x, mod[i], g1, gmlp_w_in[j].astype(BF16), gmlp_ln_g[j].reshape(1, half),
                           gmlp_ln_b[j].reshape(1, half), gmlp_w_s[j],
                           gmlp_b_s[j].reshape(GMLP_GROUPS, GMLP_BLOCK, 1),
                           gmlp_w_out[j].astype(BF16), tb=tb_gmlp)
        x = _peer_call(x.reshape(b * seq, d), mod[i], norm_g[i, 1].reshape(1, d), fg, wq_t, keys,
                       u_tab, vt_tab, layer=i, seq=seq, tb=tb_peer, ec=ec, route_lanes=route_lanes, mxu_rows=mxu_rows,
                       final_norm=(i == depth - 1)).reshape(b, seq, d)
    return x
```

```python
import functools

import jax
import jax.numpy as jnp
from jax import lax
from jax.experimental import pallas as pl
from jax.experimental.pallas import tpu as pltpu

EPS = 1e-6
CHUNK = 64
POOL_WINDOWS = (2, 4, 8, 16)
POOL_HALO = 16
CONV_WIDTH = 3
CONV_HALO = 8
GMLP_BLOCK = 128
GMLP_GROUPS = 8
PEER_HEADS = 8
PEER_TOPK = 16
ROUTE_HEADS = 2
N_MIXERS = 3

LANES = 128
SUBLANES = 8
PACKED_ROWS = 16
VMEM_LIMIT_BYTES = 56 * 1024 * 1024

BF16 = jnp.bfloat16
F32 = jnp.float32


def _rms_modulate(x, g, shift, scale):
    r = lax.rsqrt(jnp.mean(x * x, axis=-1, keepdims=True) + EPS)
    return (x * r * g) * (1.0 + scale) + shift


_GELU_K = 0.7978845608028654


def _gelu_inner(x):
    return x * (_GELU_K + (_GELU_K * 0.044715) * (x * x))


def _gelu(x):
    hx = 0.5 * x
    return hx + hx * jnp.tanh(_gelu_inner(x))


def _dot(a, b):
    return jnp.dot(a, b, preferred_element_type=F32)


def _dot_nt(a, b):
    return lax.dot_general(a, b, (((1,), (1,)), ((), ())), preferred_element_type=F32)


def _ada_kernel(c_ref, w_ref, b_ref, o_ref):
    c = c_ref[...]
    cond = (c * jax.nn.sigmoid(c)).astype(BF16)
    o_ref[...] = _dot(cond, w_ref[...].astype(BF16)) + b_ref[...]


def _ada_call(c, ada_w, ada_b):
    depth, d, n = ada_w.shape
    b = c.shape[0]
    tn = n // 4
    return pl.pallas_call(
        _ada_kernel,
        grid=(depth, n // tn),
        in_specs=[
            pl.BlockSpec((b, d), lambda l, j: (0, 0)),
            pl.BlockSpec((None, d, tn), lambda l, j: (l, 0, j)),
            pl.BlockSpec((None, 1, tn), lambda l, j: (l, 0, j)),
        ],
        out_specs=pl.BlockSpec((None, b, tn), lambda l, j: (l, 0, j)),
        out_shape=jax.ShapeDtypeStruct((depth, b, n), F32),
        compiler_params=pltpu.CompilerParams(
            dimension_semantics=("arbitrary", "arbitrary"), vmem_limit_bytes=VMEM_LIMIT_BYTES),
        name="adaln",
    )(c, ada_w, ada_b.reshape(depth, 1, n))


def _pool_kernel(x_ref, mod_ref, g_ref, win_ref, wgrp_ref, scale_ref, wout_ref, o_ref, ubuf):
    tb, d = x_ref.shape
    grp = d // len(POOL_WINDOWS)
    s = pl.program_id(1)

    @pl.when(s == 0)
    def _():
        ubuf[0:POOL_HALO, :] = jnp.zeros((POOL_HALO, d), F32)

    x = x_ref[...]
    h = _rms_modulate(x, g_ref[...], mod_ref[0], mod_ref[1]).astype(BF16)
    u = _dot(h, win_ref[...])
    ubuf[POOL_HALO:POOL_HALO + tb, :] = u

    pos = s * tb + lax.broadcasted_iota(jnp.int32, (tb, 1), 0)
    t1 = (pos + 1).astype(F32)
    zs = []
    for gi, w in enumerate(POOL_WINDOWS):
        lo = gi * grp
        acc = u[:, lo:lo + grp]
        for k in range(1, w):
            acc = acc + ubuf[POOL_HALO - k:POOL_HALO - k + tb, lo:lo + grp]
        mean = acc / jnp.minimum(t1, float(w))
        p = (mean - u[:, lo:lo + grp]).astype(BF16)
        zs.append(_dot(p, wgrp_ref[gi]))
    z = (jnp.concatenate(zs, axis=-1) * scale_ref[...]).astype(BF16)
    y = _dot(z, wout_ref[...])
    o_ref[...] = x + mod_ref[2] * y
    ubuf[0:POOL_HALO, :] = ubuf[tb:tb + POOL_HALO, :]


def _pool_call(x, mod, g, w_in, w_grp, scale, w_out, *, tb):
    b, s, d = x.shape
    ng, grp = w_grp.shape[0], w_grp.shape[1]
    return pl.pallas_call(
        _pool_kernel,
        grid=(b, s // tb),
        in_specs=[
            pl.BlockSpec((None, tb, d), lambda i, j: (i, j, 0)),
            pl.BlockSpec((None, 6, 1, d), lambda i, j: (i, 0, 0, 0)),
            pl.BlockSpec((1, d), lambda i, j: (0, 0)),
            pl.BlockSpec((d, d), lambda i, j: (0, 0)),
            pl.BlockSpec((ng, grp, grp), lambda i, j: (0, 0, 0)),
            pl.BlockSpec((1, d), lambda i, j: (0, 0)),
            pl.BlockSpec((d, d), lambda i, j: (0, 0)),
        ],
        out_specs=pl.BlockSpec((None, tb, d), lambda i, j: (i, j, 0)),
        out_shape=jax.ShapeDtypeStruct((b, s, d), F32),
        scratch_shapes=[pltpu.VMEM((tb + POOL_HALO, d), F32)],
        compiler_params=pltpu.CompilerParams(
            dimension_semantics=("arbitrary", "arbitrary"), vmem_limit_bytes=VMEM_LIMIT_BYTES),
        name="pool_mixer",
    )(x, mod, g, w_in, w_grp, scale, w_out)


def _conv_kernel(x_ref, mod_ref, g_ref, win_ref, cw_ref, wout_ref, o_ref, zbuf):
    tb, d = x_ref.shape
    s = pl.program_id(1)

    @pl.when(s == 0)
    def _():
        zbuf[0:CONV_HALO, :] = jnp.zeros((CONV_HALO, d), F32)

    x = x_ref[...]
    h = _rms_modulate(x, g_ref[...], mod_ref[0], mod_ref[1]).astype(BF16)
    proj = _dot(h, win_ref[...])
    bg, cg, xt = proj[:, 0:d], proj[:, d:2 * d], proj[:, 2 * d:3 * d]
    z = cg * xt
    zbuf[CONV_HALO:CONV_HALO + tb, :] = z
    zc = z * cw_ref[CONV_WIDTH - 1:CONV_WIDTH, :]
    for k in range(1, CONV_WIDTH):
        zc = zc + zbuf[CONV_HALO - k:CONV_HALO - k + tb, :] * cw_ref[CONV_WIDTH - 1 - k:CONV_WIDTH - k, :]
    y = _dot((bg * zc).astype(BF16), wout_ref[...])
    o_ref[...] = x + mod_ref[2] * y
    zbuf[0:CONV_HALO, :] = zbuf[tb:tb + CONV_HALO, :]


def _conv_call(x, mod, g, w_in, conv_w, w_out, *, tb):
    b, s, d = x.shape
    return pl.pallas_call(
        _conv_kernel,
        grid=(b, s // tb),
        in_specs=[
            pl.BlockSpec((None, tb, d), lambda i, j: (i, j, 0)),
            pl.BlockSpec((None, 6, 1, d), lambda i, j: (i, 0, 0, 0)),
            pl.BlockSpec((1, d), lambda i, j: (0, 0)),
            pl.BlockSpec((d, 3 * d), lambda i, j: (0, 0)),
            pl.BlockSpec((CONV_WIDTH, d), lambda i, j: (0, 0)),
            pl.BlockSpec((d, d), lambda i, j: (0, 0)),
        ],
        out_specs=pl.BlockSpec((None, tb, d), lambda i, j: (i, j, 0)),
        out_shape=jax.ShapeDtypeStruct((b, s, d), F32),
        scratch_shapes=[pltpu.VMEM((tb + CONV_HALO, d), F32)],
        compiler_params=pltpu.CompilerParams(
            dimension_semantics=("arbitrary", "arbitrary"), vmem_limit_bytes=VMEM_LIMIT_BYTES),
        name="conv_mixer",
    )(x, mod, g, w_in, conv_w, w_out)


def _gmlp_kernel(x_ref, mod_ref, g_ref, win_ref, lng_ref, lnb_ref, ws_ref, bs_ref, wout_ref, o_ref):
    tb, d = x_ref.shape
    half = wout_ref.shape[0]
    gdim = half // GMLP_GROUPS

    x = x_ref[...]
    h = _rms_modulate(x, g_ref[...], mod_ref[0], mod_ref[1]).astype(BF16)
    z = _gelu(_dot(h, win_ref[...]))
    u, v = z[:, 0:half], z[:, half:2 * half]
    mu = jnp.mean(v, axis=-1, keepdims=True)
    vc = v - mu
    var = jnp.mean(vc * vc, axis=-1, keepdims=True)
    v = ((vc * lax.rsqrt(var + EPS)) * lng_ref[...] + lnb_ref[...]).astype(BF16)

    row = lax.broadcasted_iota(jnp.int32, (GMLP_BLOCK, GMLP_BLOCK), 0) // CHUNK
    col = lax.broadcasted_iota(jnp.int32, (GMLP_BLOCK, GMLP_BLOCK), 1) // CHUNK
    causal = col <= row
    blocks = []
    for n in range(tb // GMLP_BLOCK):
        r0 = n * GMLP_BLOCK
        cols = []
        for gi in range(GMLP_GROUPS):
            ws = jnp.where(causal, ws_ref[gi], 0.0).astype(BF16)
            vs = _dot(ws, v[r0:r0 + GMLP_BLOCK, gi * gdim:(gi + 1) * gdim]) + bs_ref[gi]
            cols.append(vs)
        blocks.append(jnp.concatenate(cols, axis=-1))
    vs = jnp.concatenate(blocks, axis=0)
    y = _dot((u * vs).astype(BF16), wout_ref[...])
    o_ref[...] = x + mod_ref[2] * y


def _gmlp_call(x, mod, g, w_in, ln_g, ln_b, w_s, b_s, w_out, *, tb):
    b, s, d = x.shape
    half = w_out.shape[0]
    nblk = s // tb
    const = dict(pipeline_mode=pl.Buffered(1))
    return pl.pallas_call(
        _gmlp_kernel,
        grid=(b, nblk),
        in_specs=[
            pl.BlockSpec((None, tb, d), lambda i, j: (i, j, 0)),
            pl.BlockSpec((None, 6, 1, d), lambda i, j: (i, 0, 0, 0)),
            pl.BlockSpec((1, d), lambda i, j: (0, 0)),
            pl.BlockSpec((d, 2 * half), lambda i, j: (0, 0), **const),
            pl.BlockSpec((1, half), lambda i, j: (0, 0)),
            pl.BlockSpec((1, half), lambda i, j: (0, 0)),
            pl.BlockSpec((GMLP_GROUPS, GMLP_BLOCK, GMLP_BLOCK), lambda i, j: (0, 0, 0)),
            pl.BlockSpec((GMLP_GROUPS, GMLP_BLOCK, 1), lambda i, j: (0, 0, 0)),
            pl.BlockSpec((half, d), lambda i, j: (0, 0), **const),
        ],
        out_specs=pl.BlockSpec((None, tb, d), lambda i, j: (i, j, 0)),
        out_shape=jax.ShapeDtypeStruct((b, s, d), F32),
        compiler_params=pltpu.CompilerParams(
            dimension_semantics=("arbitrary", "arbitrary"), vmem_limit_bytes=VMEM_LIMIT_BYTES),
        name="gmlp_mixer",
    )(x, mod, g, w_in, ln_g, ln_b, w_s, b_s, w_out)


def _merge_sort_network(lo, hi):
    def merge(lo, hi, r):
        step = r * 2
        if step < hi - lo:
            yield from merge(lo, hi, step)
            yield from merge(lo + r, hi, step)
            yield from [(i, i + r) for i in range(lo + r, hi - r, step)]
        else:
            yield (lo, lo + r)

    if hi - lo >= 1:
        mid = lo + (hi - lo) // 2
        yield from _merge_sort_network(lo, mid)
        yield from _merge_sort_network(mid + 1, hi)
        yield from merge(lo, hi, 1)


def _sorted_top(s, k):
    n = s.shape[0] // SUBLANES
    assert n == k and n & (n - 1) == 0
    v = [s[g * SUBLANES:(g + 1) * SUBLANES] for g in range(n)]
    for i, j in _merge_sort_network(0, n - 1):
        v[i], v[j] = jnp.maximum(v[i], v[j]), jnp.minimum(v[i], v[j])
    rows = lax.broadcasted_iota(jnp.int32, (k, s.shape[1]), 0)
    out = jnp.zeros((k, s.shape[1]), F32)
    for it in range(k):
        m = jnp.max(v[0], axis=0, keepdims=True)
        out = jnp.where(rows == it, m, out)
        eq = v[0] == m
        depth = k - it
        for dd in range(depth - 1):
            v[dd] = jnp.where(eq, v[dd + 1], v[dd])
        v[depth - 1] = jnp.where(eq, -jnp.inf, v[depth - 1])
    return out


def _route_tile(s0, s1):
    k = PEER_TOPK
    a = _sorted_top(s0, k)
    b = _sorted_top(s1, k)
    rank1 = jnp.full(s1.shape, float(k), F32)
    for r in range(k - 1, -1, -1):
        rank1 = jnp.where(s1 == b[r:r + 1], float(r), rank1)
    parts = [a[0:1] + b]
    for r in range(1, 8):
        parts.append(a[r:r + 1] + b[0:8])
    parts.append(a[8:16] + b[0:1])
    cand = jnp.concatenate(parts, axis=0)
    top = cand[0:1]
    work = cand
    tau = top
    for _ in range(k):
        tau = jnp.max(work, axis=0, keepdims=True)
        work = jnp.where(work == tau, -jnp.inf, work)
    z = jnp.sum(jnp.where(cand >= tau, jnp.exp(cand - top), 0.0), axis=0, keepdims=True)

    cnt_r = jnp.zeros(a.shape, F32)
    for c in range(k):
        cnt_r = cnt_r + jnp.where(a + b[c:c + 1] >= tau, 1.0, 0.0)
    cnt = jnp.zeros(s0.shape, F32)
    for r in range(k):
        cnt = jnp.where(s0 == a[r:r + 1], cnt_r[r:r + 1], cnt)

    a0 = 0.5 * jnp.exp(s0 - a[0:1]) / z
    e1 = jnp.exp(s1 - b[0:1])
    return rank1, e1, cnt, a0


def _peer_kernel(x_ref, mod_ref, g_ref, fg_ref, wq_ref, keys_ref, u_ref, vt_ref, o_ref,
                 ht_scr, q_scr, col_scr, cnt_scr, a0_scr, y0_scr, y1_scr, p0_scr, p1_scr,
                 acc_scr,
                 *, route_lanes, mxu_rows, final_norm):
    tb, d = x_ref.shape
    ec = u_ref.shape[0]
    heads, _, nk, dh = keys_ref.shape
    c = pl.program_id(1)
    nc = pl.num_programs(1) - 2
    rows_per_step = ec // nk
    n_lt = tb // LANES
    n_rt = tb // route_lanes
    n_jg = nk // SUBLANES

    y_bufs = (y0_scr, y1_scr)
    p_bufs = (p0_scr, p1_scr)

    def stage_a(slot):
        part = {}

        def item(m, k):
            rows = slice(m * mxu_rows, (m + 1) * mxu_rows)
            ks = slice(k * mxu_rows, (k + 1) * mxu_rows)
            prod = _dot(u_ref[rows, ks], ht_scr[ks, :])
            part[m] = prod if k == 0 else part[m] + prod
            if k == d // mxu_rows - 1:
                y_bufs[slot][rows, :] = part.pop(m)

        return [functools.partial(item, m, k) for m in range(ec // mxu_rows)
                for k in range(d // mxu_rows)]

    def stage_b(slot):
        cb = c - 1
        y_scr, p_scr = y_bufs[slot], p_bufs[slot]

        def item(ii, lt):
            grp = cb * (rows_per_step // SUBLANES) + ii // SUBLANES
            sl = ii % SUBLANES
            lanes = slice(lt * LANES, (lt + 1) * LANES)
            cnt_rows, a0_rows = [], []
            for hd in range(heads):
                cnt_rows.append(jnp.broadcast_to(cnt_scr[hd, grp, sl:sl + 1, lanes],
                                                 (SUBLANES, LANES)))
                a0_rows.append(jnp.broadcast_to(a0_scr[hd, grp, sl:sl + 1, lanes],
                                                (SUBLANES, LANES)))
            for jp in range(n_jg // 2):
                gated = []
                for jg in (2 * jp, 2 * jp + 1):
                    w = jnp.zeros((SUBLANES, LANES), F32)
                    base = (lt * n_jg + jg) * 2 * SUBLANES
                    for hd in range(heads):
                        rank = col_scr[hd, base:base + SUBLANES, :]
                        e1 = col_scr[hd, base + SUBLANES:base + 2 * SUBLANES, :]
                        w = jnp.where(rank < cnt_rows[hd], w + a0_rows[hd] * e1, w)
                    r0 = ii * nk + jg * SUBLANES
                    y = y_scr[r0:r0 + SUBLANES, lanes]
                    wy = w * y
                    gated.append(wy + wy * jnp.tanh(_gelu_inner(y)))
                r0 = ii * nk + jp * PACKED_ROWS
                p_scr[r0:r0 + PACKED_ROWS, lanes] = jnp.concatenate(gated, axis=0).astype(BF16)

        return [functools.partial(item, ii, lt) for ii in range(rows_per_step)
                for lt in range(n_lt)]

    def stage_c(slot):
        part = {}

        def item(m, k):
            rows = slice(m * mxu_rows, (m + 1) * mxu_rows)
            ks = slice(k * mxu_rows, (k + 1) * mxu_rows)
            prod = _dot(vt_ref[rows, ks], p_bufs[slot][ks, :])
            part[m] = prod if k == 0 else part[m] + prod
            if k == ec // mxu_rows - 1:
                acc_scr[rows, :] += part.pop(m)

        return [functools.partial(item, m, k) for m in range(d // mxu_rows)
                for k in range(ec // mxu_rows)]

    def run(mxu_items, valu_items):
        n_m, n_v = len(mxu_items), len(valu_items)
        if n_m == 0 or n_v == 0:
            for f in mxu_items + valu_items:
                f()
            return
        done_v = 0
        for k, f in enumerate(mxu_items):
            f()
            upto = ((k + 1) * n_v) // n_m
            for g in valu_items[done_v:upto]:
                g()
            done_v = upto

    def interleave(a, b):
        out = []
        for k in range(max(len(a), len(b))):
            out += a[k:k + 1] + b[k:k + 1]
        return out

    @pl.when(c == 0)
    def _first():
        x = x_ref[...]
        h = _rms_modulate(x, g_ref[...], mod_ref[3], mod_ref[4])
        ht_scr[...] = h.T.astype(BF16)
        q_scr[...] = _dot(wq_ref[...], ht_scr[...]).astype(BF16)
        acc_scr[...] = jnp.zeros_like(acc_scr)

        def route_head(hd, lane_group):
            lanes = pl.ds(pl.multiple_of(lane_group * route_lanes, route_lanes), route_lanes)
            q0 = q_scr[pl.ds(pl.multiple_of(hd * 2 * dh, dh), dh), lanes]
            q1 = q_scr[pl.ds(pl.multiple_of(hd * 2 * dh + dh, dh), dh), lanes]
            s0 = _dot(keys_ref[hd, 0], q0)
            s1 = _dot(keys_ref[hd, 1], q1)
            rank1, e1, cnt, a0 = _route_tile(s0, s1)
            for ls in range(route_lanes // LANES):
                tiles = []
                for jg in range(n_jg):
                    blk = (slice(jg * SUBLANES, (jg + 1) * SUBLANES),
                           slice(ls * LANES, (ls + 1) * LANES))
                    tiles += [rank1[blk], e1[blk]]
                lt = lane_group * (route_lanes // LANES) + ls
                start = pl.multiple_of(lt * (n_jg * 2 * SUBLANES), n_jg * 2 * SUBLANES)
                col_scr[hd, pl.ds(start, n_jg * 2 * SUBLANES), :] = jnp.concatenate(tiles, axis=0)
            cnt_scr[hd, :, :, lanes] = cnt.reshape(nk // SUBLANES, SUBLANES, route_lanes)
            a0_scr[hd, :, :, lanes] = a0.reshape(nk // SUBLANES, SUBLANES, route_lanes)

        def route(it, carry):
            for k in range(ROUTE_HEADS):
                route_head((it // n_rt) * ROUTE_HEADS + k, it % n_rt)
            return carry

        lax.fori_loop(0, (heads // ROUTE_HEADS) * n_rt, route, 0)
        run(stage_a(0), [])

    @pl.when(c == 1)
    def _fill():
        run(stage_a(1), stage_b(0))

    steady = jnp.logical_and(c >= 2, c < nc)

    @pl.when(jnp.logical_and(steady, c % 2 == 0))
    def _steady_even():
        run(interleave(stage_a(0), stage_c(0)), stage_b(1))

    @pl.when(jnp.logical_and(steady, c % 2 == 1))
    def _steady_odd():
        run(interleave(stage_a(1), stage_c(1)), stage_b(0))

    @pl.when(c == nc)
    def _drain():
        run(stage_c(0), stage_b(1))

    @pl.when(c == nc + 1)
    def _last():
        run(stage_c(1), [])
        x = x_ref[...]
        out = x + mod_ref[5] * acc_scr[...].T
        if final_norm:
            r = lax.rsqrt(jnp.mean(out * out, axis=-1, keepdims=True) + EPS)
            out = out * r * fg_ref[...]
        o_ref[...] = out


def _peer_call(x2, mod, g, final_g, wq_t, keys, u_tab, vt_tab, *, layer, seq, tb, ec, route_lanes, mxu_rows,
               final_norm):
    t, d = x2.shape
    n_exp = u_tab.shape[1]
    heads, _, nk, dh = keys.shape[1:]
    tiles_per_seq = seq // tb
    nc = n_exp // ec
    assert ec % (SUBLANES * nk) == 0 and nk % PACKED_ROWS == 0 and nc >= 4 and nc % 2 == 0
    assert seq % tb == 0 and tb % route_lanes == 0 and route_lanes % LANES == 0
    kern = functools.partial(_peer_kernel, route_lanes=route_lanes, mxu_rows=mxu_rows,
                             final_norm=final_norm)
    return pl.pallas_call(
        kern,
        grid=(t // tb, nc + 2),
        in_specs=[
            pl.BlockSpec((tb, d), lambda i, c: (i, 0)),
            pl.BlockSpec((None, 6, 1, d), lambda i, c: (i // tiles_per_seq, 0, 0, 0)),
            pl.BlockSpec((1, d), lambda i, c: (0, 0)),
            pl.BlockSpec((1, d), lambda i, c: (0, 0)),
            pl.BlockSpec((None, heads * 2 * dh, d), lambda i, c: (layer, 0, 0)),
            pl.BlockSpec((None, heads, 2, nk, dh), lambda i, c: (layer, 0, 0, 0, 0)),
            pl.BlockSpec((None, ec, d), lambda i, c: (layer, jnp.minimum(c, nc - 1), 0)),
            pl.BlockSpec((None, d, ec), lambda i, c: (layer, 0, jnp.clip(c - 2, 0, nc - 1))),
        ],
        out_specs=pl.BlockSpec((tb, d), lambda i, c: (i, 0)),
        out_shape=jax.ShapeDtypeStruct((t, d), F32),
        scratch_shapes=[
            pltpu.VMEM((d, tb), BF16),
            pltpu.VMEM((heads * 2 * dh, tb), BF16),
            pltpu.VMEM((heads, (tb // LANES) * (nk // SUBLANES) * 2 * SUBLANES + SUBLANES, LANES),
                       F32),
            pltpu.VMEM((heads, nk // SUBLANES, SUBLANES, tb), F32),
            pltpu.VMEM((heads, nk // SUBLANES, SUBLANES, tb), F32),
            pltpu.VMEM((ec, tb), F32),
            pltpu.VMEM((ec, tb), F32),
            pltpu.VMEM((ec, tb), BF16),
            pltpu.VMEM((ec, tb), BF16),
            pltpu.VMEM((d, tb), F32),
        ],
        compiler_params=pltpu.CompilerParams(
            dimension_semantics=("arbitrary", "arbitrary"), vmem_limit_bytes=VMEM_LIMIT_BYTES),
        name="peer",
    )(x2, mod, g, final_g, wq_t, keys, u_tab, vt_tab)


def _tiles(seq, n_exp):
    tb_mix = min(seq, 512)
    tb_gmlp = min(seq, 256)
    tb_peer = min(seq, 512)
    ec = min(n_exp // 4, 1024)
    route_lanes = min(tb_peer, 256)
    mxu_rows = 256
    return tb_mix, tb_gmlp, tb_peer, ec, route_lanes, mxu_rows


def kernel(x, c, ada_w, ada_b, norm_g, pool_w_in, pool_w_grp, pool_scale, pool_w_out, conv_w_in, conv_w, conv_w_out, gmlp_w_in, gmlp_ln_g, gmlp_ln_b, gmlp_w_s, gmlp_b_s, gmlp_w_out, peer_w_q, peer_keys, peer_u, peer_v, final_g):
    b, seq, d = x.shape
    depth = ada_w.shape[0]
    n_exp = peer_u.shape[1]
    tb_mix, tb_gmlp, tb_peer, ec, route_lanes, mxu_rows = _tiles(seq, n_exp)

    mod = _ada_call(c, ada_w, ada_b).reshape(depth, b, 6, 1, d)

    wq_t = jnp.swapaxes(peer_w_q, 1, 2).astype(BF16)
    keys = peer_keys.astype(BF16)
    u_tab = peer_u.astype(BF16)
    vt_tab = jnp.swapaxes(peer_v, 1, 2).astype(BF16)
    fg = final_g.reshape(1, d)

    for i in range(depth):
        kind, j = i % N_MIXERS, i // N_MIXERS
        g1 = norm_g[i, 0].reshape(1, d)
        if kind == 0:
            x = _pool_call(x, mod[i], g1, pool_w_in[j].astype(BF16), pool_w_grp[j].astype(BF16),
                           pool_scale[j].reshape(1, d), pool_w_out[j].astype(BF16), tb=tb_mix)
        elif kind == 1:
            x = _conv_call(x, mod[i], g1, conv_w_in[j].astype(BF16), conv_w[j],
                           conv_w_out[j].astype(BF16), tb=tb_mix)
        else:
            half = gmlp_w_out.shape[1]
            x = _gmlp_call(x, mod[i], g1, gmlp_w_in[j].astype(BF16), gmlp_ln_g[j].reshape(1, half),
                           gmlp_ln_b[j].reshape(1, half), gmlp_w_s[j],
                           gmlp_b_s[j].reshape(GMLP_GROUPS, GMLP_BLOCK, 1),
                           gmlp_w_out[j].astype(BF16), tb=tb_gmlp)
        x = _peer_call(x.reshape(b * seq, d), mod[i], norm_g[i, 1].reshape(1, d), fg, wq_t, keys,
                       u_tab, vt_tab, layer=i, seq=seq, tb=tb_peer, ec=ec, route_lanes=route_lanes, mxu_rows=mxu_rows,
                       final_norm=(i == depth - 1)).reshape(b, seq, d)
    return x
```

```python
import functools

import jax
import jax.numpy as jnp
from jax import lax
from jax.experimental import pallas as pl
from jax.experimental.pallas import tpu as pltpu

EPS = 1e-6
CHUNK = 64
POOL_WINDOWS = (2, 4, 8, 16)
POOL_HALO = 16
CONV_WIDTH = 3
CONV_HALO = 8
GMLP_BLOCK = 128
GMLP_GROUPS = 8
PEER_HEADS = 8
PEER_TOPK = 16
ROUTE_HEADS = 2
N_MIXERS = 3

LANES = 128
SUBLANES = 8
PACKED_ROWS = 16
VMEM_LIMIT_BYTES = 56 * 1024 * 1024

BF16 = jnp.bfloat16
F32 = jnp.float32


def _rms_modulate(x, g, shift, scale):
    r = lax.rsqrt(jnp.mean(x * x, axis=-1, keepdims=True) + EPS)
    return (x * r * g) * (1.0 + scale) + shift


_GELU_K = 0.7978845608028654


def _gelu_inner(x):
    return x * (_GELU_K + (_GELU_K * 0.044715) * (x * x))


def _gelu(x):
    hx = 0.5 * x
    return hx + hx * jnp.tanh(_gelu_inner(x))


def _dot(a, b):
    return jnp.dot(a, b, preferred_element_type=F32)


def _dot_nt(a, b):
    return lax.dot_general(a, b, (((1,), (1,)), ((), ())), preferred_element_type=F32)


def _ada_kernel(c_ref, w_ref, b_ref, o_ref):
    c = c_ref[...]
    cond = (c * jax.nn.sigmoid(c)).astype(BF16)
    o_ref[...] = _dot(cond, w_ref[...].astype(BF16)) + b_ref[...]


def _ada_call(c, ada_w, ada_b):
    depth, d, n = ada_w.shape
    b = c.shape[0]
    tn = n // 4
    return pl.pallas_call(
        _ada_kernel,
        grid=(depth, n // tn),
        in_specs=[
            pl.BlockSpec((b, d), lambda l, j: (0, 0)),
            pl.BlockSpec((None, d, tn), lambda l, j: (l, 0, j)),
            pl.BlockSpec((None, 1, tn), lambda l, j: (l, 0, j)),
        ],
        out_specs=pl.BlockSpec((None, b, tn), lambda l, j: (l, 0, j)),
        out_shape=jax.ShapeDtypeStruct((depth, b, n), F32),
        compiler_params=pltpu.CompilerParams(
            dimension_semantics=("arbitrary", "arbitrary"), vmem_limit_bytes=VMEM_LIMIT_BYTES),
        name="adaln",
    )(c, ada_w, ada_b.reshape(depth, 1, n))


def _pool_kernel(x_ref, mod_ref, g_ref, win_ref, wgrp_ref, scale_ref, wout_ref, o_ref, ubuf):
    tb, d = x_ref.shape
    grp = d // len(POOL_WINDOWS)
    s = pl.program_id(1)

    @pl.when(s == 0)
    def _():
        ubuf[0:POOL_HALO, :] = jnp.zeros((POOL_HALO, d), F32)

    x = x_ref[...]
    h = _rms_modulate(x, g_ref[...], mod_ref[0], mod_ref[1]).astype(BF16)
    u = _dot(h, win_ref[...])
    ubuf[POOL_HALO:POOL_HALO + tb, :] = u

    pos = s * tb + lax.broadcasted_iota(jnp.int32, (tb, 1), 0)
    t1 = (pos + 1).astype(F32)
    zs = []
    for gi, w in enumerate(POOL_WINDOWS):
        lo = gi * grp
        acc = u[:, lo:lo + grp]
        for k in range(1, w):
            acc = acc + ubuf[POOL_HALO - k:POOL_HALO - k + tb, lo:lo + grp]
        mean = acc / jnp.minimum(t1, float(w))
        p = (mean - u[:, lo:lo + grp]).astype(BF16)
        zs.append(_dot(p, wgrp_ref[gi]))
    z = (jnp.concatenate(zs, axis=-1) * scale_ref[...]).astype(BF16)
    y = _dot(z, wout_ref[...])
    o_ref[...] = x + mod_ref[2] * y
    ubuf[0:POOL_HALO, :] = ubuf[tb:tb + POOL_HALO, :]


def _pool_call(x, mod, g, w_in, w_grp, scale, w_out, *, tb):
    b, s, d = x.shape
    ng, grp = w_grp.shape[0], w_grp.shape[1]
    return pl.pallas_call(
        _pool_kernel,
        grid=(b, s // tb),
        in_specs=[
            pl.BlockSpec((None, tb, d), lambda i, j: (i, j, 0)),
            pl.BlockSpec((None, 6, 1, d), lambda i, j: (i, 0, 0, 0)),
            pl.BlockSpec((1, d), lambda i, j: (0, 0)),
            pl.BlockSpec((d, d), lambda i, j: (0, 0)),
            pl.BlockSpec((ng, grp, grp), lambda i, j: (0, 0, 0)),
            pl.BlockSpec((1, d), lambda i, j: (0, 0)),
            pl.BlockSpec((d, d), lambda i, j: (0, 0)),
        ],
        out_specs=pl.BlockSpec((None, tb, d), lambda i, j: (i, j, 0)),
        out_shape=jax.ShapeDtypeStruct((b, s, d), F32),
        scratch_shapes=[pltpu.VMEM((tb + POOL_HALO, d), F32)],
        compiler_params=pltpu.CompilerParams(
            dimension_semantics=("arbitrary", "arbitrary"), vmem_limit_bytes=VMEM_LIMIT_BYTES),
        name="pool_mixer",
    )(x, mod, g, w_in, w_grp, scale, w_out)


def _conv_kernel(x_ref, mod_ref, g_ref, win_ref, cw_ref, wout_ref, o_ref, zbuf):
    tb, d = x_ref.shape
    s = pl.program_id(1)

    @pl.when(s == 0)
    def _():
        zbuf[0:CONV_HALO, :] = jnp.zeros((CONV_HALO, d), F32)

    x = x_ref[...]
    h = _rms_modulate(x, g_ref[...], mod_ref[0], mod_ref[1]).astype(BF16)
    proj = _dot(h, win_ref[...])
    bg, cg, xt = proj[:, 0:d], proj[:, d:2 * d], proj[:, 2 * d:3 * d]
    z = cg * xt
    zbuf[CONV_HALO:CONV_HALO + tb, :] = z
    zc = z * cw_ref[CONV_WIDTH - 1:CONV_WIDTH, :]
    for k in range(1, CONV_WIDTH):
        zc = zc + zbuf[CONV_HALO - k:CONV_HALO - k + tb, :] * cw_ref[CONV_WIDTH - 1 - k:CONV_WIDTH - k, :]
    y = _dot((bg * zc).astype(BF16), wout_ref[...])
    o_ref[...] = x + mod_ref[2] * y
    zbuf[0:CONV_HALO, :] = zbuf[tb:tb + CONV_HALO, :]


def _conv_call(x, mod, g, w_in, conv_w, w_out, *, tb):
    b, s, d = x.shape
    return pl.pallas_call(
        _conv_kernel,
        grid=(b, s // tb),
        in_specs=[
            pl.BlockSpec((None, tb, d), lambda i, j: (i, j, 0)),
            pl.BlockSpec((None, 6, 1, d), lambda i, j: (i, 0, 0, 0)),
            pl.BlockSpec((1, d), lambda i, j: (0, 0)),
            pl.BlockSpec((d, 3 * d), lambda i, j: (0, 0)),
            pl.BlockSpec((CONV_WIDTH, d), lambda i, j: (0, 0)),
            pl.BlockSpec((d, d), lambda i, j: (0, 0)),
        ],
        out_specs=pl.BlockSpec((None, tb, d), lambda i, j: (i, j, 0)),
        out_shape=jax.ShapeDtypeStruct((b, s, d), F32),
        scratch_shapes=[pltpu.VMEM((tb + CONV_HALO, d), F32)],
        compiler_params=pltpu.CompilerParams(
            dimension_semantics=("arbitrary", "arbitrary"), vmem_limit_bytes=VMEM_LIMIT_BYTES),
        name="conv_mixer",
    )(x, mod, g, w_in, conv_w, w_out)


def _gmlp_kernel(x_ref, mod_ref, g_ref, win_ref, lng_ref, lnb_ref, ws_ref, bs_ref, wout_ref, o_ref):
    tb, d = x_ref.shape
    half = wout_ref.shape[0]
    gdim = half // GMLP_GROUPS

    x = x_ref[...]
    h = _rms_modulate(x, g_ref[...], mod_ref[0], mod_ref[1]).astype(BF16)
    z = _gelu(_dot(h, win_ref[...]))
    u, v = z[:, 0:half], z[:, half:2 * half]
    mu = jnp.mean(v, axis=-1, keepdims=True)
    vc = v - mu
    var = jnp.mean(vc * vc, axis=-1, keepdims=True)
    v = ((vc * lax.rsqrt(var + EPS)) * lng_ref[...] + lnb_ref[...]).astype(BF16)

    row = lax.broadcasted_iota(jnp.int32, (GMLP_BLOCK, GMLP_BLOCK), 0) // CHUNK
    col = lax.broadcasted_iota(jnp.int32, (GMLP_BLOCK, GMLP_BLOCK), 1) // CHUNK
    causal = col <= row
    blocks = []
    for n in range(tb // GMLP_BLOCK):
        r0 = n * GMLP_BLOCK
        cols = []
        for gi in range(GMLP_GROUPS):
            ws = jnp.where(causal, ws_ref[gi], 0.0).astype(BF16)
            vs = _dot(ws, v[r0:r0 + GMLP_BLOCK, gi * gdim:(gi + 1) * gdim]) + bs_ref[gi]
            cols.append(vs)
        blocks.append(jnp.concatenate(cols, axis=-1))
    vs = jnp.concatenate(blocks, axis=0)
    y = _dot((u * vs).astype(BF16), wout_ref[...])
    o_ref[...] = x + mod_ref[2] * y


def _gmlp_call(x, mod, g, w_in, ln_g, ln_b, w_s, b_s, w_out, *, tb):
    b, s, d = x.shape
    half = w_out.shape[0]
    nblk = s // tb
    const = dict(pipeline_mode=pl.Buffered(1))
    return pl.pallas_call(
        _gmlp_kernel,
        grid=(b, nblk),
        in_specs=[
            pl.BlockSpec((None, tb, d), lambda i, j: (i, j, 0)),
            pl.BlockSpec((None, 6, 1, d), lambda i, j: (i, 0, 0, 0)),
            pl.BlockSpec((1, d), lambda i, j: (0, 0)),
            pl.BlockSpec((d, 2 * half), lambda i, j: (0, 0), **const),
            pl.BlockSpec((1, half), lambda i, j: (0, 0)),
            pl.BlockSpec((1, half), lambda i, j: (0, 0)),
            pl.BlockSpec((GMLP_GROUPS, GMLP_BLOCK, GMLP_BLOCK), lambda i, j: (0, 0, 0)),
            pl.BlockSpec((GMLP_GROUPS, GMLP_BLOCK, 1), lambda i, j: (0, 0, 0)),
            pl.BlockSpec((half, d), lambda i, j: (0, 0), **const),
        ],
        out_specs=pl.BlockSpec((None, tb, d), lambda i, j: (i, j, 0)),
        out_shape=jax.ShapeDtypeStruct((b, s, d), F32),
        compiler_params=pltpu.CompilerParams(
            dimension_semantics=("arbitrary", "arbitrary"), vmem_limit_bytes=VMEM_LIMIT_BYTES),
        name="gmlp_mixer",
    )(x, mod, g, w_in, ln_g, ln_b, w_s, b_s, w_out)


def _merge_sort_network(lo, hi):
    def merge(lo, hi, r):
        step = r * 2
        if step < hi - lo:
            yield from merge(lo, hi, step)
            yield from merge(lo + r, hi, step)
            yield from [(i, i + r) for i in range(lo + r, hi - r, step)]
        else:
            yield (lo, lo + r)

    if hi - lo >= 1:
        mid = lo + (hi - lo) // 2
        yield from _merge_sort_network(lo, mid)
        yield from _merge_sort_network(mid + 1, hi)
        yield from merge(lo, hi, 1)


def _sorted_top(s, k):
    n = s.shape[0] // SUBLANES
    assert n == k and n & (n - 1) == 0
    v = [s[g * SUBLANES:(g + 1) * SUBLANES] for g in range(n)]
    for i, j in _merge_sort_network(0, n - 1):
        v[i], v[j] = jnp.maximum(v[i], v[j]), jnp.minimum(v[i], v[j])
    rows = lax.broadcasted_iota(jnp.int32, (k, s.shape[1]), 0)
    out = jnp.zeros((k, s.shape[1]), F32)
    for it in range(k):
        m = jnp.max(v[0], axis=0, keepdims=True)
        out = jnp.where(rows == it, m, out)
        eq = v[0] == m
        depth = k - it
        for dd in range(depth - 1):
            v[dd] = jnp.where(eq, v[dd + 1], v[dd])
        v[depth - 1] = jnp.where(eq, -jnp.inf, v[depth - 1])
    return out


def _route_tile(s0, s1):
    k = PEER_TOPK
    a = _sorted_top(s0, k)
    b = _sorted_top(s1, k)
    rank1 = jnp.full(s1.shape, float(k), F32)
    for r in range(k - 1, -1, -1):
        rank1 = jnp.where(s1 == b[r:r + 1], float(r), rank1)
    parts = [a[0:1] + b]
    for r in range(1, 8):
        parts.append(a[r:r + 1] + b[0:8])
    parts.append(a[8:16] + b[0:1])
    cand = jnp.concatenate(parts, axis=0)
    top = cand[0:1]
    work = cand
    tau = top
    for _ in range(k):
        tau = jnp.max(work, axis=0, keepdims=True)
        work = jnp.where(work == tau, -jnp.inf, work)
    z = jnp.sum(jnp.where(cand >= tau, jnp.exp(cand - top), 0.0), axis=0, keepdims=True)

    cnt_r = jnp.zeros(a.shape, F32)
    for c in range(k):
        cnt_r = cnt_r + jnp.where(a + b[c:c + 1] >= tau, 1.0, 0.0)
    cnt = jnp.zeros(s0.shape, F32)
    for r in range(k):
        cnt = jnp.where(s0 == a[r:r + 1], cnt_r[r:r + 1], cnt)

    a0 = 0.5 * jnp.exp(s0 - a[0:1]) / z
    e1 = jnp.exp(s1 - b[0:1])
    return rank1, e1, cnt, a0


def _peer_kernel(x_ref, mod_ref, g_ref, fg_ref, wq_ref, keys_ref, u_ref, vt_ref, o_ref,
                 ht_scr, q_scr, col_scr, cnt_scr, a0_scr, row_scr, y0_scr, y1_scr, p0_scr, p1_scr,
                 acc_scr,
                 *, route_lanes, mxu_rows, final_norm):
    tb, d = x_ref.shape
    ec = u_ref.shape[0]
    heads, _, nk, dh = keys_ref.shape
    c = pl.program_id(1)
    nc = pl.num_programs(1) - 2
    rows_per_step = ec // nk
    n_lt = tb // LANES
    n_rt = tb // route_lanes
    n_jg = nk // SUBLANES

    y_bufs = (y0_scr, y1_scr)
    p_bufs = (p0_scr, p1_scr)

    def stage_a(slot):
        part = {}

        def item(m, k):
            rows = slice(m * mxu_rows, (m + 1) * mxu_rows)
            ks = slice(k * mxu_rows, (k + 1) * mxu_rows)
            prod = _dot(u_ref[rows, ks], ht_scr[ks, :])
            part[m] = prod if k == 0 else part[m] + prod
            if k == d // mxu_rows - 1:
                y_bufs[slot][rows, :] = part.pop(m)

        return [functools.partial(item, m, k) for m in range(ec // mxu_rows)
                for k in range(d // mxu_rows)]

    def stage_b(slot):
        cb = c - 1
        y_scr, p_scr = y_bufs[slot], p_bufs[slot]

        def item(jg, lt):
            lanes = slice(lt * LANES, (lt + 1) * LANES)
            base = (lt * n_jg + jg) * 2 * SUBLANES
            ranks = [col_scr[hd, base:base + SUBLANES, :] for hd in range(heads)]
            e1s = [col_scr[hd, base + SUBLANES:base + 2 * SUBLANES, :] for hd in range(heads)]
            for ii in range(rows_per_step):
                irow = cb * rows_per_step + ii
                w = jnp.zeros((SUBLANES, LANES), F32)
                for hd in range(heads):
                    cnt = jnp.broadcast_to(row_scr[irow, lt, hd:hd + 1, :], (SUBLANES, LANES))
                    a0 = jnp.broadcast_to(row_scr[irow, lt, heads + hd:heads + hd + 1, :],
                                          (SUBLANES, LANES))
                    w = jnp.where(ranks[hd] < cnt, w + a0 * e1s[hd], w)
                r0 = ii * nk + jg * SUBLANES
                y = y_scr[r0:r0 + SUBLANES, lanes]
                wy = w * y
                p_scr[r0:r0 + SUBLANES, lanes] = (wy + wy * jnp.tanh(_gelu_inner(y))).astype(BF16)

        return [functools.partial(item, jg, lt) for jg in range(n_jg) for lt in range(n_lt)]

    def stage_c(slot):
        part = {}

        def item(m, k):
            rows = slice(m * mxu_rows, (m + 1) * mxu_rows)
            ks = slice(k * mxu_rows, (k + 1) * mxu_rows)
            prod = _dot(vt_ref[rows, ks], p_bufs[slot][ks, :])
            part[m] = prod if k == 0 else part[m] + prod
            if k == ec // mxu_rows - 1:
                acc_scr[rows, :] += part.pop(m)

        return [functools.partial(item, m, k) for m in range(d // mxu_rows)
                for k in range(ec // mxu_rows)]

    def run(mxu_items, valu_items):
        n_m, n_v = len(mxu_items), len(valu_items)
        if n_m == 0 or n_v == 0:
            for f in mxu_items + valu_items:
                f()
            return
        done_v = 0
        for k, f in enumerate(mxu_items):
            f()
            upto = ((k + 1) * n_v) // n_m
            for g in valu_items[done_v:upto]:
                g()
            done_v = upto

    def interleave(a, b):
        out = []
        for k in range(max(len(a), len(b))):
            out += a[k:k + 1] + b[k:k + 1]
        return out

    @pl.when(c == 0)
    def _first():
        x = x_ref[...]
        h = _rms_modulate(x, g_ref[...], mod_ref[3], mod_ref[4])
        ht_scr[...] = h.T.astype(BF16)
        q_scr[...] = _dot(wq_ref[...], ht_scr[...]).astype(BF16)
        acc_scr[...] = jnp.zeros_like(acc_scr)

        def route_head(hd, lane_group):
            lanes = pl.ds(pl.multiple_of(lane_group * route_lanes, route_lanes), route_lanes)
            q0 = q_scr[pl.ds(pl.multiple_of(hd * 2 * dh, dh), dh), lanes]
            q1 = q_scr[pl.ds(pl.multiple_of(hd * 2 * dh + dh, dh), dh), lanes]
            s0 = _dot(keys_ref[hd, 0], q0)
            s1 = _dot(keys_ref[hd, 1], q1)
            rank1, e1, cnt, a0 = _route_tile(s0, s1)
            for ls in range(route_lanes // LANES):
                tiles = []
                for jg in range(n_jg):
                    blk = (slice(jg * SUBLANES, (jg + 1) * SUBLANES),
                           slice(ls * LANES, (ls + 1) * LANES))
                    tiles += [rank1[blk], e1[blk]]
                lt = lane_group * (route_lanes // LANES) + ls
                start = pl.multiple_of(lt * (n_jg * 2 * SUBLANES), n_jg * 2 * SUBLANES)
                col_scr[hd, pl.ds(start, n_jg * 2 * SUBLANES), :] = jnp.concatenate(tiles, axis=0)
            cnt_scr[hd, :, :, lanes] = cnt.reshape(nk // SUBLANES, SUBLANES, route_lanes)
            a0_scr[hd, :, :, lanes] = a0.reshape(nk // SUBLANES, SUBLANES, route_lanes)

        def route(it, carry):
            for k in range(ROUTE_HEADS):
                route_head((it // n_rt) * ROUTE_HEADS + k, it % n_rt)
            return carry

        lax.fori_loop(0, (heads // ROUTE_HEADS) * n_rt, route, 0)

        def regroup(grp, carry):
            sub = lax.broadcasted_iota(jnp.int32, (SUBLANES, LANES), 0)
            for lt in range(n_lt):
                lanes = slice(lt * LANES, (lt + 1) * LANES)
                for sl in range(SUBLANES):
                    for k, src in enumerate((cnt_scr, a0_scr)):
                        tile = jnp.zeros((SUBLANES, LANES), F32)
                        for hd in range(heads):
                            row = jnp.broadcast_to(src[hd, grp, sl:sl + 1, lanes], (SUBLANES, LANES))
                            tile = jnp.where(sub == hd, row, tile)
                        row_scr[grp * SUBLANES + sl, lt, k * heads:(k + 1) * heads, :] = tile
            return carry

        lax.fori_loop(0, nk // SUBLANES, regroup, 0)
        run(stage_a(0), [])

    @pl.when(c == 1)
    def _fill():
        run(stage_a(1), stage_b(0))

    steady = jnp.logical_and(c >= 2, c < nc)

    @pl.when(jnp.logical_and(steady, c % 2 == 0))
    def _steady_even():
        run(interleave(stage_a(0), stage_c(0)), stage_b(1))

    @pl.when(jnp.logical_and(steady, c % 2 == 1))
    def _steady_odd():
        run(interleave(stage_a(1), stage_c(1)), stage_b(0))

    @pl.when(c == nc)
    def _drain():
        run(stage_c(0), stage_b(1))

    @pl.when(c == nc + 1)
    def _last():
        run(stage_c(1), [])
        x = x_ref[...]
        out = x + mod_ref[5] * acc_scr[...].T
        if final_norm:
            r = lax.rsqrt(jnp.mean(out * out, axis=-1, keepdims=True) + EPS)
            out = out * r * fg_ref[...]
        o_ref[...] = out


def _peer_call(x2, mod, g, final_g, wq_t, keys, u_tab, vt_tab, *, layer, seq, tb, ec, route_lanes, mxu_rows,
               final_norm):
    t, d = x2.shape
    n_exp = u_tab.shape[1]
    heads, _, nk, dh = keys.shape[1:]
    tiles_per_seq = seq // tb
    nc = n_exp // ec
    assert ec % (SUBLANES * nk) == 0 and nk % PACKED_ROWS == 0 and nc >= 4 and nc % 2 == 0
    assert heads == SUBLANES and heads % ROUTE_HEADS == 0
    assert seq % tb == 0 and tb % route_lanes == 0 and route_lanes % LANES == 0
    kern = functools.partial(_peer_kernel, route_lanes=route_lanes, mxu_rows=mxu_rows,
                             final_norm=final_norm)
    return pl.pallas_call(
        kern,
        grid=(t // tb, nc + 2),
        in_specs=[
            pl.BlockSpec((tb, d), lambda i, c: (i, 0)),
            pl.BlockSpec((None, 6, 1, d), lambda i, c: (i // tiles_per_seq, 0, 0, 0)),
            pl.BlockSpec((1, d), lambda i, c: (0, 0)),
            pl.BlockSpec((1, d), lambda i, c: (0, 0)),
            pl.BlockSpec((None, heads * 2 * dh, d), lambda i, c: (layer, 0, 0)),
            pl.BlockSpec((None, heads, 2, nk, dh), lambda i, c: (layer, 0, 0, 0, 0)),
            pl.BlockSpec((None, ec, d), lambda i, c: (layer, jnp.minimum(c, nc - 1), 0)),
            pl.BlockSpec((None, d, ec), lambda i, c: (layer, 0, jnp.clip(c - 2, 0, nc - 1))),
        ],
        out_specs=pl.BlockSpec((tb, d), lambda i, c: (i, 0)),
        out_shape=jax.ShapeDtypeStruct((t, d), F32),
        scratch_shapes=[
            pltpu.VMEM((d, tb), BF16),
            pltpu.VMEM((heads * 2 * dh, tb), BF16),
            pltpu.VMEM((heads, (tb // LANES) * (nk // SUBLANES) * 2 * SUBLANES + SUBLANES, LANES),
                       F32),
            pltpu.VMEM((heads, nk // SUBLANES, SUBLANES, tb), F32),
            pltpu.VMEM((heads, nk // SUBLANES, SUBLANES, tb), F32),
            pltpu.VMEM((nk, tb // LANES, 2 * heads, LANES), F32),
            pltpu.VMEM((ec, tb), F32),
            pltpu.VMEM((ec, tb), F32),
            pltpu.VMEM((ec, tb), BF16),
            pltpu.VMEM((ec, tb), BF16),
            pltpu.VMEM((d, tb), F32),
        ],
        compiler_params=pltpu.CompilerParams(
            dimension_semantics=("arbitrary", "arbitrary"), vmem_limit_bytes=VMEM_LIMIT_BYTES),
        name="peer",
    )(x2, mod, g, final_g, wq_t, keys, u_tab, vt_tab)


def _tiles(seq, n_exp):
    tb_mix = min(seq, 512)
    tb_gmlp = min(seq, 256)
    tb_peer = min(seq, 512)
    ec = min(n_exp // 4, 1024)
    route_lanes = min(tb_peer, 256)
    mxu_rows = 256
    return tb_mix, tb_gmlp, tb_peer, ec, route_lanes, mxu_rows


def kernel(x, c, ada_w, ada_b, norm_g, pool_w_in, pool_w_grp, pool_scale, pool_w_out, conv_w_in, conv_w, conv_w_out, gmlp_w_in, gmlp_ln_g, gmlp_ln_b, gmlp_w_s, gmlp_b_s, gmlp_w_out, peer_w_q, peer_keys, peer_u, peer_v, final_g):
    b, seq, d = x.shape
    depth = ada_w.shape[0]
    n_exp = peer_u.shape[1]
    tb_mix, tb_gmlp, tb_peer, ec, route_lanes, mxu_rows = _tiles(seq, n_exp)

    mod = _ada_call(c, ada_w, ada_b).reshape(depth, b, 6, 1, d)

    wq_t = jnp.swapaxes(peer_w_q, 1, 2).astype(BF16)
    keys = peer_keys.astype(BF16)
    u_tab = peer_u.astype(BF16)
    vt_tab = jnp.swapaxes(peer_v, 1, 2).astype(BF16)
    fg = final_g.reshape(1, d)

    for i in range(depth):
        kind, j = i % N_MIXERS, i // N_MIXERS
        g1 = norm_g[i, 0].reshape(1, d)
        if kind == 0:
            x = _pool_call(x, mod[i], g1, pool_w_in[j].astype(BF16), pool_w_grp[j].astype(BF16),
                           pool_scale[j].reshape(1, d), pool_w_out[j].astype(BF16), tb=tb_mix)
        elif kind == 1:
            x = _conv_call(x, mod[i], g1, conv_w_in[j].astype(BF16), conv_w[j],
                           conv_w_out[j].astype(BF16), tb=tb_mix)
        else:
            half = gmlp_w_out.shape[1]
            x = _gmlp_call(x, mod[i], g1, gmlp_w_in[j].astype(BF16), gmlp_ln_g[j].reshape(1, half),
                           gmlp_ln_b[j].reshape(1, half), gmlp_w_s[j],
                           gmlp_b_s[j].reshape(GMLP_GROUPS, GMLP_BLOCK, 1),
                           gmlp_w_out[j].astype(BF16), tb=tb_gmlp)
        x = _peer_call(x.reshape(b * seq, d), mod[i], norm_g[i, 1].reshape(1, d), fg, wq_t, keys,
                       u_tab, vt_tab, layer=i, seq=seq, tb=tb_peer, ec=ec, route_lanes=route_lanes, mxu_rows=mxu_rows,
                       final_norm=(i == depth - 1)).reshape(b, seq, d)
    return x
```

```python
import functools

import jax
import jax.numpy as jnp
from jax import lax
from jax.experimental import pallas as pl
from jax.experimental.pallas import tpu as pltpu

EPS = 1e-6
CHUNK = 64
POOL_WINDOWS = (2, 4, 8, 16)
POOL_HALO = 16
CONV_WIDTH = 3
CONV_HALO = 8
GMLP_BLOCK = 128
GMLP_GROUPS = 8
PEER_HEADS = 8
PEER_TOPK = 16
ROUTE_HEADS = 2
RANK_BITS = 5
RANK_MASK = (1 << RANK_BITS) - 1
N_MIXERS = 3

LANES = 128
SUBLANES = 8
PACKED_ROWS = 16
VMEM_LIMIT_BYTES = 56 * 1024 * 1024

BF16 = jnp.bfloat16
F32 = jnp.float32


def _rms_modulate(x, g, shift, scale):
    r = lax.rsqrt(jnp.mean(x * x, axis=-1, keepdims=True) + EPS)
    return (x * r * g) * (1.0 + scale) + shift


_GELU_K = 0.7978845608028654


def _gelu_inner(x):
    return x * (_GELU_K + (_GELU_K * 0.044715) * (x * x))


def _gelu(x):
    hx = 0.5 * x
    return hx + hx * jnp.tanh(_gelu_inner(x))


def _dot(a, b):
    return jnp.dot(a, b, preferred_element_type=F32)


def _dot_nt(a, b):
    return lax.dot_general(a, b, (((1,), (1,)), ((), ())), preferred_element_type=F32)


def _ada_kernel(c_ref, w_ref, b_ref, o_ref):
    c = c_ref[...]
    cond = (c * jax.nn.sigmoid(c)).astype(BF16)
    o_ref[...] = _dot(cond, w_ref[...].astype(BF16)) + b_ref[...]


def _ada_call(c, ada_w, ada_b):
    depth, d, n = ada_w.shape
    b = c.shape[0]
    tn = n // 4
    return pl.pallas_call(
        _ada_kernel,
        grid=(depth, n // tn),
        in_specs=[
            pl.BlockSpec((b, d), lambda l, j: (0, 0)),
            pl.BlockSpec((None, d, tn), lambda l, j: (l, 0, j)),
            pl.BlockSpec((None, 1, tn), lambda l, j: (l, 0, j)),
        ],
        out_specs=pl.BlockSpec((None, b, tn), lambda l, j: (l, 0, j)),
        out_shape=jax.ShapeDtypeStruct((depth, b, n), F32),
        compiler_params=pltpu.CompilerParams(
            dimension_semantics=("arbitrary", "arbitrary"), vmem_limit_bytes=VMEM_LIMIT_BYTES),
        name="adaln",
    )(c, ada_w, ada_b.reshape(depth, 1, n))


def _pool_kernel(x_ref, mod_ref, g_ref, win_ref, wgrp_ref, scale_ref, wout_ref, o_ref, ubuf):
    tb, d = x_ref.shape
    grp = d // len(POOL_WINDOWS)
    s = pl.program_id(1)

    @pl.when(s == 0)
    def _():
        ubuf[0:POOL_HALO, :] = jnp.zeros((POOL_HALO, d), F32)

    x = x_ref[...]
    h = _rms_modulate(x, g_ref[...], mod_ref[0], mod_ref[1]).astype(BF16)
    u = _dot(h, win_ref[...])
    ubuf[POOL_HALO:POOL_HALO + tb, :] = u

    pos = s * tb + lax.broadcasted_iota(jnp.int32, (tb, 1), 0)
    t1 = (pos + 1).astype(F32)
    zs = []
    for gi, w in enumerate(POOL_WINDOWS):
        lo = gi * grp
        acc = u[:, lo:lo + grp]
        for k in range(1, w):
            acc = acc + ubuf[POOL_HALO - k:POOL_HALO - k + tb, lo:lo + grp]
        mean = acc / jnp.minimum(t1, float(w))
        p = (mean - u[:, lo:lo + grp]).astype(BF16)
        zs.append(_dot(p, wgrp_ref[gi]))
    z = (jnp.concatenate(zs, axis=-1) * scale_ref[...]).astype(BF16)
    y = _dot(z, wout_ref[...])
    o_ref[...] = x + mod_ref[2] * y
    ubuf[0:POOL_HALO, :] = ubuf[tb:tb + POOL_HALO, :]


def _pool_call(x, mod, g, w_in, w_grp, scale, w_out, *, tb):
    b, s, d = x.shape
    ng, grp = w_grp.shape[0], w_grp.shape[1]
    return pl.pallas_call(
        _pool_kernel,
        grid=(b, s // tb),
        in_specs=[
            pl.BlockSpec((None, tb, d), lambda i, j: (i, j, 0)),
            pl.BlockSpec((None, 6, 1, d), lambda i, j: (i, 0, 0, 0)),
            pl.BlockSpec((1, d), lambda i, j: (0, 0)),
            pl.BlockSpec((d, d), lambda i, j: (0, 0)),
            pl.BlockSpec((ng, grp, grp), lambda i, j: (0, 0, 0)),
            pl.BlockSpec((1, d), lambda i, j: (0, 0)),
            pl.BlockSpec((d, d), lambda i, j: (0, 0)),
        ],
        out_specs=pl.BlockSpec((None, tb, d), lambda i, j: (i, j, 0)),
        out_shape=jax.ShapeDtypeStruct((b, s, d), F32),
        scratch_shapes=[pltpu.VMEM((tb + POOL_HALO, d), F32)],
        compiler_params=pltpu.CompilerParams(
            dimension_semantics=("arbitrary", "arbitrary"), vmem_limit_bytes=VMEM_LIMIT_BYTES),
        name="pool_mixer",
    )(x, mod, g, w_in, w_grp, scale, w_out)


def _conv_kernel(x_ref, mod_ref, g_ref, win_ref, cw_ref, wout_ref, o_ref, zbuf):
    tb, d = x_ref.shape
    s = pl.program_id(1)

    @pl.when(s == 0)
    def _():
        zbuf[0:CONV_HALO, :] = jnp.zeros((CONV_HALO, d), F32)

    x = x_ref[...]
    h = _rms_modulate(x, g_ref[...], mod_ref[0], mod_ref[1]).astype(BF16)
    proj = _dot(h, win_ref[...])
    bg, cg, xt = proj[:, 0:d], proj[:, d:2 * d], proj[:, 2 * d:3 * d]
    z = cg * xt
    zbuf[CONV_HALO:CONV_HALO + tb, :] = z
    zc = z * cw_ref[CONV_WIDTH - 1:CONV_WIDTH, :]
    for k in range(1, CONV_WIDTH):
        zc = zc + zbuf[CONV_HALO - k:CONV_HALO - k + tb, :] * cw_ref[CONV_WIDTH - 1 - k:CONV_WIDTH - k, :]
    y = _dot((bg * zc).astype(BF16), wout_ref[...])
    o_ref[...] = x + mod_ref[2] * y
    zbuf[0:CONV_HALO, :] = zbuf[tb:tb + CONV_HALO, :]


def _conv_call(x, mod, g, w_in, conv_w, w_out, *, tb):
    b, s, d = x.shape
    return pl.pallas_call(
        _conv_kernel,
        grid=(b, s // tb),
        in_specs=[
            pl.BlockSpec((None, tb, d), lambda i, j: (i, j, 0)),
            pl.BlockSpec((None, 6, 1, d), lambda i, j: (i, 0, 0, 0)),
            pl.BlockSpec((1, d), lambda i, j: (0, 0)),
            pl.BlockSpec((d, 3 * d), lambda i, j: (0, 0)),
            pl.BlockSpec((CONV_WIDTH, d), lambda i, j: (0, 0)),
            pl.BlockSpec((d, d), lambda i, j: (0, 0)),
        ],
        out_specs=pl.BlockSpec((None, tb, d), lambda i, j: (i, j, 0)),
        out_shape=jax.ShapeDtypeStruct((b, s, d), F32),
        scratch_shapes=[pltpu.VMEM((tb + CONV_HALO, d), F32)],
        compiler_params=pltpu.CompilerParams(
            dimension_semantics=("arbitrary", "arbitrary"), vmem_limit_bytes=VMEM_LIMIT_BYTES),
        name="conv_mixer",
    )(x, mod, g, w_in, conv_w, w_out)


def _gmlp_kernel(x_ref, mod_ref, g_ref, win_ref, lng_ref, lnb_ref, ws_ref, bs_ref, wout_ref, o_ref):
    tb, d = x_ref.shape
    half = wout_ref.shape[0]
    gdim = half // GMLP_GROUPS

    x = x_ref[...]
    h = _rms_modulate(x, g_ref[...], mod_ref[0], mod_ref[1]).astype(BF16)
    z = _gelu(_dot(h, win_ref[...]))
    u, v = z[:, 0:half], z[:, half:2 * half]
    mu = jnp.mean(v, axis=-1, keepdims=True)
    vc = v - mu
    var = jnp.mean(vc * vc, axis=-1, keepdims=True)
    v = ((vc * lax.rsqrt(var + EPS)) * lng_ref[...] + lnb_ref[...]).astype(BF16)

    row = lax.broadcasted_iota(jnp.int32, (GMLP_BLOCK, GMLP_BLOCK), 0) // CHUNK
    col = lax.broadcasted_iota(jnp.int32, (GMLP_BLOCK, GMLP_BLOCK), 1) // CHUNK
    causal = col <= row
    blocks = []
    for n in range(tb // GMLP_BLOCK):
        r0 = n * GMLP_BLOCK
        cols = []
        for gi in range(GMLP_GROUPS):
            ws = jnp.where(causal, ws_ref[gi], 0.0).astype(BF16)
            vs = _dot(ws, v[r0:r0 + GMLP_BLOCK, gi * gdim:(gi + 1) * gdim]) + bs_ref[gi]
            cols.append(vs)
        blocks.append(jnp.concatenate(cols, axis=-1))
    vs = jnp.concatenate(blocks, axis=0)
    y = _dot((u * vs).astype(BF16), wout_ref[...])
    o_ref[...] = x + mod_ref[2] * y


def _gmlp_call(x, mod, g, w_in, ln_g, ln_b, w_s, b_s, w_out, *, tb):
    b, s, d = x.shape
    half = w_out.shape[0]
    nblk = s // tb
    const = dict(pipeline_mode=pl.Buffered(1))
    return pl.pallas_call(
        _gmlp_kernel,
        grid=(b, nblk),
        in_specs=[
            pl.BlockSpec((None, tb, d), lambda i, j: (i, j, 0)),
            pl.BlockSpec((None, 6, 1, d), lambda i, j: (i, 0, 0, 0)),
            pl.BlockSpec((1, d), lambda i, j: (0, 0)),
            pl.BlockSpec((d, 2 * half), lambda i, j: (0, 0), **const),
            pl.BlockSpec((1, half), lambda i, j: (0, 0)),
            pl.BlockSpec((1, half), lambda i, j: (0, 0)),
            pl.BlockSpec((GMLP_GROUPS, GMLP_BLOCK, GMLP_BLOCK), lambda i, j: (0, 0, 0)),
            pl.BlockSpec((GMLP_GROUPS, GMLP_BLOCK, 1), lambda i, j: (0, 0, 0)),
            pl.BlockSpec((half, d), lambda i, j: (0, 0), **const),
        ],
        out_specs=pl.BlockSpec((None, tb, d), lambda i, j: (i, j, 0)),
        out_shape=jax.ShapeDtypeStruct((b, s, d), F32),
        compiler_params=pltpu.CompilerParams(
            dimension_semantics=("arbitrary", "arbitrary"), vmem_limit_bytes=VMEM_LIMIT_BYTES),
        name="gmlp_mixer",
    )(x, mod, g, w_in, ln_g, ln_b, w_s, b_s, w_out)


def _merge_sort_network(lo, hi):
    def merge(lo, hi, r):
        step = r * 2
        if step < hi - lo:
            yield from merge(lo, hi, step)
            yield from merge(lo + r, hi, step)
            yield from [(i, i + r) for i in range(lo + r, hi - r, step)]
        else:
            yield (lo, lo + r)

    if hi - lo >= 1:
        mid = lo + (hi - lo) // 2
        yield from _merge_sort_network(lo, mid)
        yield from _merge_sort_network(mid + 1, hi)
        yield from merge(lo, hi, 1)


def _sorted_top(s, k):
    n = s.shape[0] // SUBLANES
    assert n == k and n & (n - 1) == 0
    v = [s[g * SUBLANES:(g + 1) * SUBLANES] for g in range(n)]
    for i, j in _merge_sort_network(0, n - 1):
        v[i], v[j] = jnp.maximum(v[i], v[j]), jnp.minimum(v[i], v[j])
    rows = lax.broadcasted_iota(jnp.int32, (k, s.shape[1]), 0)
    out = jnp.zeros((k, s.shape[1]), F32)
    for it in range(k):
        m = jnp.max(v[0], axis=0, keepdims=True)
        out = jnp.where(rows == it, m, out)
        eq = v[0] == m
        depth = k - it
        for dd in range(depth - 1):
            v[dd] = jnp.where(eq, v[dd + 1], v[dd])
        v[depth - 1] = jnp.where(eq, -jnp.inf, v[depth - 1])
    return out


def _route_tile(s0, s1):
    k = PEER_TOPK
    a = _sorted_top(s0, k)
    b = _sorted_top(s1, k)
    rank1 = jnp.full(s1.shape, float(k), F32)
    for r in range(k - 1, -1, -1):
        rank1 = jnp.where(s1 == b[r:r + 1], float(r), rank1)
    parts = [a[0:1] + b]
    for r in range(1, 8):
        parts.append(a[r:r + 1] + b[0:8])
    parts.append(a[8:16] + b[0:1])
    cand = jnp.concatenate(parts, axis=0)
    top = cand[0:1]
    work = cand
    tau = top
    for _ in range(k):
        tau = jnp.max(work, axis=0, keepdims=True)
        work = jnp.where(work == tau, -jnp.inf, work)
    z = jnp.sum(jnp.where(cand >= tau, jnp.exp(cand - top), 0.0), axis=0, keepdims=True)

    cnt_r = jnp.zeros(a.shape, F32)
    for c in range(k):
        cnt_r = cnt_r + jnp.where(a + b[c:c + 1] >= tau, 1.0, 0.0)
    cnt = jnp.zeros(s0.shape, F32)
    for r in range(k):
        cnt = jnp.where(s0 == a[r:r + 1], cnt_r[r:r + 1], cnt)

    a0 = 0.5 * jnp.exp(s0 - a[0:1]) / z
    e1 = jnp.exp(s1 - b[0:1])
    return rank1, e1, cnt, a0


def _peer_kernel(x_ref, mod_ref, g_ref, fg_ref, wq_ref, keys_ref, u_ref, vt_ref, o_ref,
                 ht_scr, q_scr, col_scr, cnt_scr, a0_scr, y0_scr, y1_scr, p0_scr, p1_scr,
                 acc_scr,
                 *, route_lanes, mxu_rows, final_norm):
    tb, d = x_ref.shape
    ec = u_ref.shape[0]
    heads, _, nk, dh = keys_ref.shape
    c = pl.program_id(1)
    nc = pl.num_programs(1) - 2
    rows_per_step = ec // nk
    n_lt = tb // LANES
    n_rt = tb // route_lanes
    n_jg = nk // SUBLANES

    y_bufs = (y0_scr, y1_scr)
    p_bufs = (p0_scr, p1_scr)

    def stage_a(slot):
        part = {}

        def item(m, k):
            rows = slice(m * mxu_rows, (m + 1) * mxu_rows)
            ks = slice(k * mxu_rows, (k + 1) * mxu_rows)
            prod = _dot(u_ref[rows, ks], ht_scr[ks, :])
            part[m] = prod if k == 0 else part[m] + prod
            if k == d // mxu_rows - 1:
                y_bufs[slot][rows, :] = part.pop(m)

        return [functools.partial(item, m, k) for m in range(ec // mxu_rows)
                for k in range(d // mxu_rows)]

    def stage_b(slot):
        cb = c - 1
        y_scr, p_scr = y_bufs[slot], p_bufs[slot]

        def item(ii, lt):
            grp = cb * (rows_per_step // SUBLANES) + ii // SUBLANES
            sl = ii % SUBLANES
            lanes = slice(lt * LANES, (lt + 1) * LANES)
            cnt_rows, a0_rows = [], []
            for hd in range(heads):
                cnt_rows.append(jnp.broadcast_to(cnt_scr[hd, grp, sl:sl + 1, lanes],
                                                 (SUBLANES, LANES)).astype(jnp.int32))
                a0_rows.append(jnp.broadcast_to(a0_scr[hd, grp, sl:sl + 1, lanes],
                                                (SUBLANES, LANES)))
            for jp in range(n_jg // 2):
                gated = []
                for jg in (2 * jp, 2 * jp + 1):
                    w = jnp.zeros((SUBLANES, LANES), F32)
                    base = (lt * n_jg + jg) * SUBLANES
                    for hd in range(heads):
                        e1 = col_scr[hd, base:base + SUBLANES, :]
                        rank = pltpu.bitcast(e1, jnp.int32) & RANK_MASK
                        w = jnp.where(rank < cnt_rows[hd], w + a0_rows[hd] * e1, w)
                    r0 = ii * nk + jg * SUBLANES
                    y = y_scr[r0:r0 + SUBLANES, lanes]
                    wy = w * y
                    gated.append(wy + wy * jnp.tanh(_gelu_inner(y)))
                r0 = ii * nk + jp * PACKED_ROWS
                p_scr[r0:r0 + PACKED_ROWS, lanes] = jnp.concatenate(gated, axis=0).astype(BF16)

        return [functools.partial(item, ii, lt) for ii in range(rows_per_step)
                for lt in range(n_lt)]

    def stage_c(slot):
        part = {}

        def item(m, k):
            rows = slice(m * mxu_rows, (m + 1) * mxu_rows)
            ks = slice(k * mxu_rows, (k + 1) * mxu_rows)
            prod = _dot(vt_ref[rows, ks], p_bufs[slot][ks, :])
            part[m] = prod if k == 0 else part[m] + prod
            if k == ec // mxu_rows - 1:
                acc_scr[rows, :] += part.pop(m)

        return [functools.partial(item, m, k) for m in range(d // mxu_rows)
                for k in range(ec // mxu_rows)]

    def run(mxu_items, valu_items):
        n_m, n_v = len(mxu_items), len(valu_items)
        if n_m == 0 or n_v == 0:
            for f in mxu_items + valu_items:
                f()
            return
        done_v = 0
        for k, f in enumerate(mxu_items):
            f()
            upto = ((k + 1) * n_v) // n_m
            for g in valu_items[done_v:upto]:
                g()
            done_v = upto

    def interleave(a, b):
        out = []
        for k in range(max(len(a), len(b))):
            out += a[k:k + 1] + b[k:k + 1]
        return out

    @pl.when(c == 0)
    def _first():
        x = x_ref[...]
        h = _rms_modulate(x, g_ref[...], mod_ref[3], mod_ref[4])
        ht_scr[...] = h.T.astype(BF16)
        q_scr[...] = _dot(wq_ref[...], ht_scr[...]).astype(BF16)
        acc_scr[...] = jnp.zeros_like(acc_scr)

        def route_head(hd, lane_group):
            lanes = pl.ds(pl.multiple_of(lane_group * route_lanes, route_lanes), route_lanes)
            q0 = q_scr[pl.ds(pl.multiple_of(hd * 2 * dh, dh), dh), lanes]
            q1 = q_scr[pl.ds(pl.multiple_of(hd * 2 * dh + dh, dh), dh), lanes]
            s0 = _dot(keys_ref[hd, 0], q0)
            s1 = _dot(keys_ref[hd, 1], q1)
            rank1, e1, cnt, a0 = _route_tile(s0, s1)
            packed = pltpu.bitcast(
                (pltpu.bitcast(e1, jnp.int32) & ~RANK_MASK) | rank1.astype(jnp.int32), F32)
            for ls in range(route_lanes // LANES):
                lt = lane_group * (route_lanes // LANES) + ls
                start = pl.multiple_of(lt * nk, nk)
                col_scr[hd, pl.ds(start, nk), :] = packed[:, ls * LANES:(ls + 1) * LANES]
            cnt_scr[hd, :, :, lanes] = cnt.reshape(nk // SUBLANES, SUBLANES, route_lanes)
            a0_scr[hd, :, :, lanes] = a0.reshape(nk // SUBLANES, SUBLANES, route_lanes)

        def route(it, carry):
            for k in range(ROUTE_HEADS):
                route_head((it // n_rt) * ROUTE_HEADS + k, it % n_rt)
            return carry

        lax.fori_loop(0, (heads // ROUTE_HEADS) * n_rt, route, 0)
        run(stage_a(0), [])

    @pl.when(c == 1)
    def _fill():
        run(stage_a(1), stage_b(0))

    steady = jnp.logical_and(c >= 2, c < nc)

    @pl.when(jnp.logical_and(steady, c % 2 == 0))
    def _steady_even():
        run(interleave(stage_a(0), stage_c(0)), stage_b(1))

    @pl.when(jnp.logical_and(steady, c % 2 == 1))
    def _steady_odd():
        run(interleave(stage_a(1), stage_c(1)), stage_b(0))

    @pl.when(c == nc)
    def _drain():
        run(stage_c(0), stage_b(1))

    @pl.when(c == nc + 1)
    def _last():
        run(stage_c(1), [])
        x = x_ref[...]
        out = x + mod_ref[5] * acc_scr[...].T
        if final_norm:
            r = lax.rsqrt(jnp.mean(out * out, axis=-1, keepdims=True) + EPS)
            out = out * r * fg_ref[...]
        o_ref[...] = out


def _peer_call(x2, mod, g, final_g, wq_t, keys, u_tab, vt_tab, *, layer, seq, tb, ec, route_lanes, mxu_rows,
               final_norm):
    t, d = x2.shape
    n_exp = u_tab.shape[1]
    heads, _, nk, dh = keys.shape[1:]
    tiles_per_seq = seq // tb
    nc = n_exp // ec
    assert ec % (SUBLANES * nk) == 0 and nk % PACKED_ROWS == 0 and nc >= 4 and nc % 2 == 0
    assert heads % ROUTE_HEADS == 0 and PEER_TOPK <= RANK_MASK
    assert seq % tb == 0 and tb % route_lanes == 0 and route_lanes % LANES == 0
    kern = functools.partial(_peer_kernel, route_lanes=route_lanes, mxu_rows=mxu_rows,
                             final_norm=final_norm)
    return pl.pallas_call(
        kern,
        grid=(t // tb, nc + 2),
        in_specs=[
            pl.BlockSpec((tb, d), lambda i, c: (i, 0)),
            pl.BlockSpec((None, 6, 1, d), lambda i, c: (i // tiles_per_seq, 0, 0, 0)),
            pl.BlockSpec((1, d), lambda i, c: (0, 0)),
            pl.BlockSpec((1, d), lambda i, c: (0, 0)),
            pl.BlockSpec((None, heads * 2 * dh, d), lambda i, c: (layer, 0, 0)),
            pl.BlockSpec((None, heads, 2, nk, dh), lambda i, c: (layer, 0, 0, 0, 0)),
            pl.BlockSpec((None, ec, d), lambda i, c: (layer, jnp.minimum(c, nc - 1), 0)),
            pl.BlockSpec((None, d, ec), lambda i, c: (layer, 0, jnp.clip(c - 2, 0, nc - 1))),
        ],
        out_specs=pl.BlockSpec((tb, d), lambda i, c: (i, 0)),
        out_shape=jax.ShapeDtypeStruct((t, d), F32),
        scratch_shapes=[
            pltpu.VMEM((d, tb), BF16),
            pltpu.VMEM((heads * 2 * dh, tb), BF16),
            pltpu.VMEM((heads, (tb // LANES) * nk + SUBLANES, LANES), F32),
            pltpu.VMEM((heads, nk // SUBLANES, SUBLANES, tb), F32),
            pltpu.VMEM((heads, nk // SUBLANES, SUBLANES, tb), F32),
            pltpu.VMEM((ec, tb), F32),
            pltpu.VMEM((ec, tb), F32),
            pltpu.VMEM((ec, tb), BF16),
            pltpu.VMEM((ec, tb), BF16),
            pltpu.VMEM((d, tb), F32),
        ],
        compiler_params=pltpu.CompilerParams(
            dimension_semantics=("arbitrary", "arbitrary"), vmem_limit_bytes=VMEM_LIMIT_BYTES),
        name="peer",
    )(x2, mod, g, final_g, wq_t, keys, u_tab, vt_tab)


def _tiles(seq, n_exp):
    tb_mix = min(seq, 512)
    tb_gmlp = min(seq, 256)
    tb_peer = min(seq, 512)
    ec = min(n_exp // 4, 1024)
    route_lanes = min(tb_peer, 256)
    mxu_rows = 256
    return tb_mix, tb_gmlp, tb_peer, ec, route_lanes, mxu_rows


def kernel(x, c, ada_w, ada_b, norm_g, pool_w_in, pool_w_grp, pool_scale, pool_w_out, conv_w_in, conv_w, conv_w_out, gmlp_w_in, gmlp_ln_g, gmlp_ln_b, gmlp_w_s, gmlp_b_s, gmlp_w_out, peer_w_q, peer_keys, peer_u, peer_v, final_g):
    b, seq, d = x.shape
    depth = ada_w.shape[0]
    n_exp = peer_u.shape[1]
    tb_mix, tb_gmlp, tb_peer, ec, route_lanes, mxu_rows = _tiles(seq, n_exp)

    mod = _ada_call(c, ada_w, ada_b).reshape(depth, b, 6, 1, d)

    wq_t = jnp.swapaxes(peer_w_q, 1, 2).astype(BF16)
    keys = peer_keys.astype(BF16)
    u_tab = peer_u.astype(BF16)
    vt_tab = jnp.swapaxes(peer_v, 1, 2).astype(BF16)
    fg = final_g.reshape(1, d)

    for i in range(depth):
        kind, j = i % N_MIXERS, i // N_MIXERS
        g1 = norm_g[i, 0].reshape(1, d)
        if kind == 0:
            x = _pool_call(x, mod[i], g1, pool_w_in[j].astype(BF16), pool_w_grp[j].astype(BF16),
                           pool_scale[j].reshape(1, d), pool_w_out[j].astype(BF16), tb=tb_mix)
        elif kind == 1:
            x = _conv_call(x, mod[i], g1, conv_w_in[j].astype(BF16), conv_w[j],
                           conv_w_out[j].astype(BF16), tb=tb_mix)
        else:
            half = gmlp_w_out.shape[1]
            x = _gmlp_call(x, mod[i], g1, gmlp_w_in[j].astype(BF16), gmlp_ln_g[j].reshape(1, half),
                           gmlp_ln_b[j].reshape(1, half), gmlp_w_s[j],
                           gmlp_b_s[j].reshape(GMLP_GROUPS, GMLP_BLOCK, 1),
                           gmlp_w_out[j].astype(BF16), tb=tb_gmlp)
        x = _peer_call(x.reshape(b * seq, d), mod[i], norm_g[i, 1].reshape(1, d), fg, wq_t, keys,
                       u_tab, vt_tab, layer=i, seq=seq, tb=tb_peer, ec=ec, route_lanes=route_lanes, mxu_rows=mxu_rows,
                       final_norm=(i == depth - 1)).reshape(b, seq, d)
    return x
```

```python
import functools

import jax
import jax.numpy as jnp
from jax import lax
from jax.experimental import pallas as pl
from jax.experimental.pallas import tpu as pltpu

EPS = 1e-6
CHUNK = 64
POOL_WINDOWS = (2, 4, 8, 16)
POOL_HALO = 16
CONV_WIDTH = 3
CONV_HALO = 8
GMLP_BLOCK = 128
GMLP_GROUPS = 8
PEER_HEADS = 8
PEER_TOPK = 16
ROUTE_HEADS = 4
N_MIXERS = 3

LANES = 128
SUBLANES = 8
PACKED_ROWS = 16
VMEM_LIMIT_BYTES = 56 * 1024 * 1024

BF16 = jnp.bfloat16
F32 = jnp.float32


def _rms_modulate(x, g, shift, scale):
    r = lax.rsqrt(jnp.mean(x * x, axis=-1, keepdims=True) + EPS)
    return (x * r * g) * (1.0 + scale) + shift


_GELU_K = 0.7978845608028654


def _gelu_inner(x):
    return x * (_GELU_K + (_GELU_K * 0.044715) * (x * x))


def _gelu(x):
    hx = 0.5 * x
    return hx + hx * jnp.tanh(_gelu_inner(x))


def _dot(a, b):
    return jnp.dot(a, b, preferred_element_type=F32)


def _dot_nt(a, b):
    return lax.dot_general(a, b, (((1,), (1,)), ((), ())), preferred_element_type=F32)


def _ada_kernel(c_ref, w_ref, b_ref, o_ref):
    c = c_ref[...]
    cond = (c * jax.nn.sigmoid(c)).astype(BF16)
    o_ref[...] = _dot(cond, w_ref[...].astype(BF16)) + b_ref[...]


def _ada_call(c, ada_w, ada_b):
    depth, d, n = ada_w.shape
    b = c.shape[0]
    tn = n // 4
    return pl.pallas_call(
        _ada_kernel,
        grid=(depth, n // tn),
        in_specs=[
            pl.BlockSpec((b, d), lambda l, j: (0, 0)),
            pl.BlockSpec((None, d, tn), lambda l, j: (l, 0, j)),
            pl.BlockSpec((None, 1, tn), lambda l, j: (l, 0, j)),
        ],
        out_specs=pl.BlockSpec((None, b, tn), lambda l, j: (l, 0, j)),
        out_shape=jax.ShapeDtypeStruct((depth, b, n), F32),
        compiler_params=pltpu.CompilerParams(
            dimension_semantics=("arbitrary", "arbitrary"), vmem_limit_bytes=VMEM_LIMIT_BYTES),
        name="adaln",
    )(c, ada_w, ada_b.reshape(depth, 1, n))


def _pool_kernel(x_ref, mod_ref, g_ref, win_ref, wgrp_ref, scale_ref, wout_ref, o_ref, ubuf):
    tb, d = x_ref.shape
    grp = d // len(POOL_WINDOWS)
    s = pl.program_id(1)

    @pl.when(s == 0)
    def _():
        ubuf[0:POOL_HALO, :] = jnp.zeros((POOL_HALO, d), F32)

    x = x_ref[...]
    h = _rms_modulate(x, g_ref[...], mod_ref[0], mod_ref[1]).astype(BF16)
    u = _dot(h, win_ref[...])
    ubuf[POOL_HALO:POOL_HALO + tb, :] = u

    pos = s * tb + lax.broadcasted_iota(jnp.int32, (tb, 1), 0)
    t1 = (pos + 1).astype(F32)
    zs = []
    for gi, w in enumerate(POOL_WINDOWS):
        lo = gi * grp
        acc = u[:, lo:lo + grp]
        for k in range(1, w):
            acc = acc + ubuf[POOL_HALO - k:POOL_HALO - k + tb, lo:lo + grp]
        mean = acc / jnp.minimum(t1, float(w))
        p = (mean - u[:, lo:lo + grp]).astype(BF16)
        zs.append(_dot(p, wgrp_ref[gi]))
    z = (jnp.concatenate(zs, axis=-1) * scale_ref[...]).astype(BF16)
    y = _dot(z, wout_ref[...])
    o_ref[...] = x + mod_ref[2] * y
    ubuf[0:POOL_HALO, :] = ubuf[tb:tb + POOL_HALO, :]


def _pool_call(x, mod, g, w_in, w_grp, scale, w_out, *, tb):
    b, s, d = x.shape
    ng, grp = w_grp.shape[0], w_grp.shape[1]
    return pl.pallas_call(
        _pool_kernel,
        grid=(b, s // tb),
        in_specs=[
            pl.BlockSpec((None, tb, d), lambda i, j: (i, j, 0)),
            pl.BlockSpec((None, 6, 1, d), lambda i, j: (i, 0, 0, 0)),
            pl.BlockSpec((1, d), lambda i, j: (0, 0)),
            pl.BlockSpec((d, d), lambda i, j: (0, 0)),
            pl.BlockSpec((ng, grp, grp), lambda i, j: (0, 0, 0)),
            pl.BlockSpec((1, d), lambda i, j: (0, 0)),
            pl.BlockSpec((d, d), lambda i, j: (0, 0)),
        ],
        out_specs=pl.BlockSpec((None, tb, d), lambda i, j: (i, j, 0)),
        out_shape=jax.ShapeDtypeStruct((b, s, d), F32),
        scratch_shapes=[pltpu.VMEM((tb + POOL_HALO, d), F32)],
        compiler_params=pltpu.CompilerParams(
            dimension_semantics=("arbitrary", "arbitrary"), vmem_limit_bytes=VMEM_LIMIT_BYTES),
        name="pool_mixer",
    )(x, mod, g, w_in, w_grp, scale, w_out)


def _conv_kernel(x_ref, mod_ref, g_ref, win_ref, cw_ref, wout_ref, o_ref, zbuf):
    tb, d = x_ref.shape
    s = pl.program_id(1)

    @pl.when(s == 0)
    def _():
        zbuf[0:CONV_HALO, :] = jnp.zeros((CONV_HALO, d), F32)

    x = x_ref[...]
    h = _rms_modulate(x, g_ref[...], mod_ref[0], mod_ref[1]).astype(BF16)
    proj = _dot(h, win_ref[...])
    bg, cg, xt = proj[:, 0:d], proj[:, d:2 * d], proj[:, 2 * d:3 * d]
    z = cg * xt
    zbuf[CONV_HALO:CONV_HALO + tb, :] = z
    zc = z * cw_ref[CONV_WIDTH - 1:CONV_WIDTH, :]
    for k in range(1, CONV_WIDTH):
        zc = zc + zbuf[CONV_HALO - k:CONV_HALO - k + tb, :] * cw_ref[CONV_WIDTH - 1 - k:CONV_WIDTH - k, :]
    y = _dot((bg * zc).astype(BF16), wout_ref[...])
    o_ref[...] = x + mod_ref[2] * y
    zbuf[0:CONV_HALO, :] = zbuf[tb:tb + CONV_HALO, :]


def _conv_call(x, mod, g, w_in, conv_w, w_out, *, tb):
    b, s, d = x.shape
    return pl.pallas_call(
        _conv_kernel,
        grid=(b, s // tb),
        in_specs=[
            pl.BlockSpec((None, tb, d), lambda i, j: (i, j, 0)),
            pl.BlockSpec((None, 6, 1, d), lambda i, j: (i, 0, 0, 0)),
            pl.BlockSpec((1, d), lambda i, j: (0, 0)),
            pl.BlockSpec((d, 3 * d), lambda i, j: (0, 0)),
            pl.BlockSpec((CONV_WIDTH, d), lambda i, j: (0, 0)),
            pl.BlockSpec((d, d), lambda i, j: (0, 0)),
        ],
        out_specs=pl.BlockSpec((None, tb, d), lambda i, j: (i, j, 0)),
        out_shape=jax.ShapeDtypeStruct((b, s, d), F32),
        scratch_shapes=[pltpu.VMEM((tb + CONV_HALO, d), F32)],
        compiler_params=pltpu.CompilerParams(
            dimension_semantics=("arbitrary", "arbitrary"), vmem_limit_bytes=VMEM_LIMIT_BYTES),
        name="conv_mixer",
    )(x, mod, g, w_in, conv_w, w_out)


def _gmlp_kernel(x_ref, mod_ref, g_ref, win_ref, lng_ref, lnb_ref, ws_ref, bs_ref, wout_ref, o_ref):
    tb, d = x_ref.shape
    half = wout_ref.shape[0]
    gdim = half // GMLP_GROUPS

    x = x_ref[...]
    h = _rms_modulate(x, g_ref[...], mod_ref[0], mod_ref[1]).astype(BF16)
    z = _gelu(_dot(h, win_ref[...]))
    u, v = z[:, 0:half], z[:, half:2 * half]
    mu = jnp.mean(v, axis=-1, keepdims=True)
    vc = v - mu
    var = jnp.mean(vc * vc, axis=-1, keepdims=True)
    v = ((vc * lax.rsqrt(var + EPS)) * lng_ref[...] + lnb_ref[...]).astype(BF16)

    row = lax.broadcasted_iota(jnp.int32, (GMLP_BLOCK, GMLP_BLOCK), 0) // CHUNK
    col = lax.broadcasted_iota(jnp.int32, (GMLP_BLOCK, GMLP_BLOCK), 1) // CHUNK
    causal = col <= row
    blocks = []
    for n in range(tb // GMLP_BLOCK):
        r0 = n * GMLP_BLOCK
        cols = []
        for gi in range(GMLP_GROUPS):
            ws = jnp.where(causal, ws_ref[gi], 0.0).astype(BF16)
            vs = _dot(ws, v[r0:r0 + GMLP_BLOCK, gi * gdim:(gi + 1) * gdim]) + bs_ref[gi]
            cols.append(vs)
        blocks.append(jnp.concatenate(cols, axis=-1))
    vs = jnp.concatenate(blocks, axis=0)
    y = _dot((u * vs).astype(BF16), wout_ref[...])
    o_ref[...] = x + mod_ref[2] * y


def _gmlp_call(x, mod, g, w_in, ln_g, ln_b, w_s, b_s, w_out, *, tb):
    b, s, d = x.shape
    half = w_out.shape[0]
    nblk = s // tb
    const = dict(pipeline_mode=pl.Buffered(1))
    return pl.pallas_call(
        _gmlp_kernel,
        grid=(b, nblk),
        in_specs=[
            pl.BlockSpec((None, tb, d), lambda i, j: (i, j, 0)),
            pl.BlockSpec((None, 6, 1, d), lambda i, j: (i, 0, 0, 0)),
            pl.BlockSpec((1, d), lambda i, j: (0, 0)),
            pl.BlockSpec((d, 2 * half), lambda i, j: (0, 0), **const),
            pl.BlockSpec((1, half), lambda i, j: (0, 0)),
            pl.BlockSpec((1, half), lambda i, j: (0, 0)),
            pl.BlockSpec((GMLP_GROUPS, GMLP_BLOCK, GMLP_BLOCK), lambda i, j: (0, 0, 0)),
            pl.BlockSpec((GMLP_GROUPS, GMLP_BLOCK, 1), lambda i, j: (0, 0, 0)),
            pl.BlockSpec((half, d), lambda i, j: (0, 0), **const),
        ],
        out_specs=pl.BlockSpec((None, tb, d), lambda i, j: (i, j, 0)),
        out_shape=jax.ShapeDtypeStruct((b, s, d), F32),
        compiler_params=pltpu.CompilerParams(
            dimension_semantics=("arbitrary", "arbitrary"), vmem_limit_bytes=VMEM_LIMIT_BYTES),
        name="gmlp_mixer",
    )(x, mod, g, w_in, ln_g, ln_b, w_s, b_s, w_out)


def _merge_sort_network(lo, hi):
    def merge(lo, hi, r):
        step = r * 2
        if step < hi - lo:
            yield from merge(lo, hi, step)
            yield from merge(lo + r, hi, step)
            yield from [(i, i + r) for i in range(lo + r, hi - r, step)]
        else:
            yield (lo, lo + r)

    if hi - lo >= 1:
        mid = lo + (hi - lo) // 2
        yield from _merge_sort_network(lo, mid)
        yield from _merge_sort_network(mid + 1, hi)
        yield from merge(lo, hi, 1)


def _sorted_top(s, k):
    n = s.shape[0] // SUBLANES
    assert n == k and n & (n - 1) == 0
    v = [s[g * SUBLANES:(g + 1) * SUBLANES] for g in range(n)]
    for i, j in _merge_sort_network(0, n - 1):
        v[i], v[j] = jnp.maximum(v[i], v[j]), jnp.minimum(v[i], v[j])
    rows = lax.broadcasted_iota(jnp.int32, (k, s.shape[1]), 0)
    out = jnp.zeros((k, s.shape[1]), F32)
    for it in range(k):
        m = jnp.max(v[0], axis=0, keepdims=True)
        out = jnp.where(rows == it, m, out)
        eq = v[0] == m
        depth = k - it
        for dd in range(depth - 1):
            v[dd] = jnp.where(eq, v[dd + 1], v[dd])
        v[depth - 1] = jnp.where(eq, -jnp.inf, v[depth - 1])
    return out


def _route_tile(s0, s1):
    k = PEER_TOPK
    a = _sorted_top(s0, k)
    b = _sorted_top(s1, k)
    rank1 = jnp.full(s1.shape, float(k), F32)
    for r in range(k - 1, -1, -1):
        rank1 = jnp.where(s1 == b[r:r + 1], float(r), rank1)
    parts = [a[0:1] + b]
    for r in range(1, 8):
        parts.append(a[r:r + 1] + b[0:8])
    parts.append(a[8:16] + b[0:1])
    cand = jnp.concatenate(parts, axis=0)
    top = cand[0:1]
    work = cand
    tau = top
    for _ in range(k):
        tau = jnp.max(work, axis=0, keepdims=True)
        work = jnp.where(work == tau, -jnp.inf, work)
    z = jnp.sum(jnp.where(cand >= tau, jnp.exp(cand - top), 0.0), axis=0, keepdims=True)

    cnt_r = jnp.zeros(a.shape, F32)
    for c in range(k):
        cnt_r = cnt_r + jnp.where(a + b[c:c + 1] >= tau, 1.0, 0.0)
    cnt = jnp.zeros(s0.shape, F32)
    for r in range(k):
        cnt = jnp.where(s0 == a[r:r + 1], cnt_r[r:r + 1], cnt)

    a0 = 0.5 * jnp.exp(s0 - a[0:1]) / z
    e1 = jnp.exp(s1 - b[0:1])
    return rank1, e1, cnt, a0


def _peer_kernel(x_ref, mod_ref, g_ref, fg_ref, wq_ref, keys_ref, u_ref, vt_ref, o_ref,
                 ht_scr, q_scr, col_scr, cnt_scr, a0_scr, y0_scr, y1_scr, p0_scr, p1_scr,
                 acc_scr,
                 *, route_lanes, mxu_rows, final_norm):
    tb, d = x_ref.shape
    ec = u_ref.shape[0]
    heads, _, nk, dh = keys_ref.shape
    c = pl.program_id(1)
    nc = pl.num_programs(1) - 2
    rows_per_step = ec // nk
    n_lt = tb // LANES
    n_rt = tb // route_lanes
    n_jg = nk // SUBLANES

    y_bufs = (y0_scr, y1_scr)
    p_bufs = (p0_scr, p1_scr)

    def stage_a(slot):
        part = {}

        def item(m, k):
            rows = slice(m * mxu_rows, (m + 1) * mxu_rows)
            ks = slice(k * mxu_rows, (k + 1) * mxu_rows)
            prod = _dot(u_ref[rows, ks], ht_scr[ks, :])
            part[m] = prod if k == 0 else part[m] + prod
            if k == d // mxu_rows - 1:
                y_bufs[slot][rows, :] = part.pop(m)

        return [functools.partial(item, m, k) for m in range(ec // mxu_rows)
                for k in range(d // mxu_rows)]

    def stage_b(slot):
        cb = c - 1
        y_scr, p_scr = y_bufs[slot], p_bufs[slot]

        def item(ii, lt):
            grp = cb * (rows_per_step // SUBLANES) + ii // SUBLANES
            sl = ii % SUBLANES
            lanes = slice(lt * LANES, (lt + 1) * LANES)
            cnt_rows, a0_rows = [], []
            for hd in range(heads):
                cnt_rows.append(jnp.broadcast_to(cnt_scr[hd, grp, sl:sl + 1, lanes],
                                                 (SUBLANES, LANES)))
                a0_rows.append(jnp.broadcast_to(a0_scr[hd, grp, sl:sl + 1, lanes],
                                                (SUBLANES, LANES)))
            for jp in range(n_jg // 2):
                gated = []
                for jg in (2 * jp, 2 * jp + 1):
                    w = None
                    base = (lt * n_jg + jg) * 2 * SUBLANES
                    for hd in range(heads):
                        rank = col_scr[hd, base:base + SUBLANES, :]
                        e1 = col_scr[hd, base + SUBLANES:base + 2 * SUBLANES, :]
                        p = a0_rows[hd] * e1
                        if w is None:
                            w = jnp.where(rank < cnt_rows[hd], p, 0.0)
                        else:
                            w = jnp.where(rank < cnt_rows[hd], w + p, w)
                    r0 = ii * nk + jg * SUBLANES
                    y = y_scr[r0:r0 + SUBLANES, lanes]
                    wy = w * y
                    gated.append(wy + wy * jnp.tanh(_gelu_inner(y)))
                r0 = ii * nk + jp * PACKED_ROWS
                p_scr[r0:r0 + PACKED_ROWS, lanes] = jnp.concatenate(gated, axis=0).astype(BF16)

        return [functools.partial(item, ii, lt) for ii in range(rows_per_step)
                for lt in range(n_lt)]

    def stage_c(slot):
        part = {}

        def item(m, k):
            rows = slice(m * mxu_rows, (m + 1) * mxu_rows)
            ks = slice(k * mxu_rows, (k + 1) * mxu_rows)
            prod = _dot(vt_ref[rows, ks], p_bufs[slot][ks, :])
            part[m] = prod if k == 0 else part[m] + prod
            if k == ec // mxu_rows - 1:
                acc_scr[rows, :] += part.pop(m)

        return [functools.partial(item, m, k) for m in range(d // mxu_rows)
                for k in range(ec // mxu_rows)]

    def run(mxu_items, valu_items):
        n_m, n_v = len(mxu_items), len(valu_items)
        if n_m == 0 or n_v == 0:
            for f in mxu_items + valu_items:
                f()
            return
        done_v = 0
        for k, f in enumerate(mxu_items):
            f()
            upto = ((k + 1) * n_v) // n_m
            for g in valu_items[done_v:upto]:
                g()
            done_v = upto

    def interleave(a, b):
        out = []
        for k in range(max(len(a), len(b))):
            out += a[k:k + 1] + b[k:k + 1]
        return out

    @pl.when(c == 0)
    def _first():
        x = x_ref[...]
        h = _rms_modulate(x, g_ref[...], mod_ref[3], mod_ref[4])
        ht_scr[...] = h.T.astype(BF16)
        q_scr[...] = _dot(wq_ref[...], ht_scr[...]).astype(BF16)
        acc_scr[...] = jnp.zeros_like(acc_scr)

        def route_head(hd, lane_group):
            lanes = pl.ds(pl.multiple_of(lane_group * route_lanes, route_lanes), route_lanes)
            q0 = q_scr[pl.ds(pl.multiple_of(hd * 2 * dh, dh), dh), lanes]
            q1 = q_scr[pl.ds(pl.multiple_of(hd * 2 * dh + dh, dh), dh), lanes]
            s0 = _dot(keys_ref[hd, 0], q0)
            s1 = _dot(keys_ref[hd, 1], q1)
            rank1, e1, cnt, a0 = _route_tile(s0, s1)
            for ls in range(route_lanes // LANES):
                tiles = []
                for jg in range(n_jg):
                    blk = (slice(jg * SUBLANES, (jg + 1) * SUBLANES),
                           slice(ls * LANES, (ls + 1) * LANES))
                    tiles += [rank1[blk], e1[blk]]
                lt = lane_group * (route_lanes // LANES) + ls
                start = pl.multiple_of(lt * (n_jg * 2 * SUBLANES), n_jg * 2 * SUBLANES)
                col_scr[hd, pl.ds(start, n_jg * 2 * SUBLANES), :] = jnp.concatenate(tiles, axis=0)
            cnt_scr[hd, :, :, lanes] = cnt.reshape(nk // SUBLANES, SUBLANES, route_lanes)
            a0_scr[hd, :, :, lanes] = a0.reshape(nk // SUBLANES, SUBLANES, route_lanes)

        def route(it, carry):
            for k in range(ROUTE_HEADS):
                route_head((it // n_rt) * ROUTE_HEADS + k, it % n_rt)
            return carry

        lax.fori_loop(0, (heads // ROUTE_HEADS) * n_rt, route, 0)
        run(stage_a(0), [])

    @pl.when(c == 1)
    def _fill():
        run(stage_a(1), stage_b(0))

    steady = jnp.logical_and(c >= 2, c < nc)

    @pl.when(jnp.logical_and(steady, c % 2 == 0))
    def _steady_even():
        run(interleave(stage_a(0), stage_c(0)), stage_b(1))

    @pl.when(jnp.logical_and(steady, c % 2 == 1))
    def _steady_odd():
        run(interleave(stage_a(1), stage_c(1)), stage_b(0))

    @pl.when(c == nc)
    def _drain():
        run(stage_c(0), stage_b(1))

    @pl.when(c == nc + 1)
    def _last():
        run(stage_c(1), [])
        x = x_ref[...]
        out = x + mod_ref[5] * acc_scr[...].T
        if final_norm:
            r = lax.rsqrt(jnp.mean(out * out, axis=-1, keepdims=True) + EPS)
            out = out * r * fg_ref[...]
        o_ref[...] = out


def _peer_call(x2, mod, g, final_g, wq_t, keys, u_tab, vt_tab, *, layer, seq, tb, ec, route_lanes, mxu_rows,
               final_norm):
    t, d = x2.shape
    n_exp = u_tab.shape[1]
    heads, _, nk, dh = keys.shape[1:]
    tiles_per_seq = seq // tb
    nc = n_exp // ec
    assert ec % (SUBLANES * nk) == 0 and nk % PACKED_ROWS == 0 and nc >= 4 and nc % 2 == 0
    assert seq % tb == 0 and tb % route_lanes == 0 and route_lanes % LANES == 0
    kern = functools.partial(_peer_kernel, route_lanes=route_lanes, mxu_rows=mxu_rows,
                             final_norm=final_norm)
    return pl.pallas_call(
        kern,
        grid=(t // tb, nc + 2),
        in_specs=[
            pl.BlockSpec((tb, d), lambda i, c: (i, 0)),
            pl.BlockSpec((None, 6, 1, d), lambda i, c: (i // tiles_per_seq, 0, 0, 0)),
            pl.BlockSpec((1, d), lambda i, c: (0, 0)),
            pl.BlockSpec((1, d), lambda i, c: (0, 0)),
            pl.BlockSpec((None, heads * 2 * dh, d), lambda i, c: (layer, 0, 0)),
            pl.BlockSpec((None, heads, 2, nk, dh), lambda i, c: (layer, 0, 0, 0, 0)),
            pl.BlockSpec((None, ec, d), lambda i, c: (layer, jnp.minimum(c, nc - 1), 0)),
            pl.BlockSpec((None, d, ec), lambda i, c: (layer, 0, jnp.clip(c - 2, 0, nc - 1))),
        ],
        out_specs=pl.BlockSpec((tb, d), lambda i, c: (i, 0)),
        out_shape=jax.ShapeDtypeStruct((t, d), F32),
        scratch_shapes=[
            pltpu.VMEM((d, tb), BF16),
            pltpu.VMEM((heads * 2 * dh, tb), BF16),
            pltpu.VMEM((heads, (tb // LANES) * (nk // SUBLANES) * 2 * SUBLANES + SUBLANES, LANES),
                       F32),
            pltpu.VMEM((heads, nk // SUBLANES, SUBLANES, tb), F32),
            pltpu.VMEM((heads, nk // SUBLANES, SUBLANES, tb), F32),
            pltpu.VMEM((ec, tb), F32),
            pltpu.VMEM((ec, tb), F32),
            pltpu.VMEM((ec, tb), BF16),
            pltpu.VMEM((ec, tb), BF16),
            pltpu.VMEM((d, tb), F32),
        ],
        compiler_params=pltpu.CompilerParams(
            dimension_semantics=("arbitrary", "arbitrary"), vmem_limit_bytes=VMEM_LIMIT_BYTES),
        name="peer",
    )(x2, mod, g, final_g, wq_t, keys, u_tab, vt_tab)


def _tiles(seq, n_exp):
    tb_mix = min(seq, 512)
    tb_gmlp = min(seq, 256)
    tb_peer = min(seq, 512)
    ec = min(n_exp // 4, 1024)
    route_lanes = min(tb_peer, 256)
    mxu_rows = 256
    return tb_mix, tb_gmlp, tb_peer, ec, route_lanes, mxu_rows


def kernel(x, c, ada_w, ada_b, norm_g, pool_w_in, pool_w_grp, pool_scale, pool_w_out, conv_w_in, conv_w, conv_w_out, gmlp_w_in, gmlp_ln_g, gmlp_ln_b, gmlp_w_s, gmlp_b_s, gmlp_w_out, peer_w_q, peer_keys, peer_u, peer_v, final_g):
    b, seq, d = x.shape
    depth = ada_w.shape[0]
    n_exp = peer_u.shape[1]
    tb_mix, tb_gmlp, tb_peer, ec, route_lanes, mxu_rows = _tiles(seq, n_exp)

    mod = _ada_call(c, ada_w, ada_b).reshape(depth, b, 6, 1, d)

    wq_t = jnp.swapaxes(peer_w_q, 1, 2).astype(BF16)
    keys = peer_keys.astype(BF16)
    u_tab = peer_u.astype(BF16)
    vt_tab = jnp.swapaxes(peer_v, 1, 2).astype(BF16)
    fg = final_g.reshape(1, d)

    for i in range(depth):
        kind, j = i % N_MIXERS, i // N_MIXERS
        g1 = norm_g[i, 0].reshape(1, d)
        if kind == 0:
            x = _pool_call(x, mod[i], g1, pool_w_in[j].astype(BF16), pool_w_grp[j].astype(BF16),
                           pool_scale[j].reshape(1, d), pool_w_out[j].astype(BF16), tb=tb_mix)
        elif kind == 1:
            x = _conv_call(x, mod[i], g1, conv_w_in[j].astype(BF16), conv_w[j],
                           conv_w_out[j].astype(BF16), tb=tb_mix)
        else:
            half = gmlp_w_out.shape[1]
            x = _gmlp_call(x, mod[i], g1, gmlp_w_in[j].astype(BF16), gmlp_ln_g[j].reshape(1, half),
                           gmlp_ln_b[j].reshape(1, half), gmlp_w_s[j],
                           gmlp_b_s[j].reshape(GMLP_GROUPS, GMLP_BLOCK, 1),
                           gmlp_w_out[j].astype(BF16), tb=tb_gmlp)
        x = _peer_call(x.reshape(b * seq, d), mod[i], norm_g[i, 1].reshape(1, d), fg, wq_t, keys,
                       u_tab, vt_tab, layer=i, seq=seq, tb=tb_peer, ec=ec, route_lanes=route_lanes, mxu_rows=mxu_rows,
                       final_norm=(i == depth - 1)).reshape(b, seq, d)
    return x
```

```python
import functools

import jax
import jax.numpy as jnp
from jax import lax
from jax.experimental import pallas as pl
from jax.experimental.pallas import tpu as pltpu

EPS = 1e-6
CHUNK = 64
POOL_WINDOWS = (2, 4, 8, 16)
POOL_HALO = 16
CONV_WIDTH = 3
CONV_HALO = 8
GMLP_BLOCK = 128
GMLP_GROUPS = 8
PEER_HEADS = 8
PEER_TOPK = 16
ROUTE_HEADS = 4
N_MIXERS = 3

LANES = 128
SUBLANES = 8
PACKED_ROWS = 16
VMEM_LIMIT_BYTES = 56 * 1024 * 1024

BF16 = jnp.bfloat16
F32 = jnp.float32


def _rms_modulate(x, g, shift, scale):
    r = lax.rsqrt(jnp.mean(x * x, axis=-1, keepdims=True) + EPS)
    return (x * r * g) * (1.0 + scale) + shift


_GELU_K = 0.7978845608028654


def _gelu_inner(x):
    return x * (_GELU_K + (_GELU_K * 0.044715) * (x * x))


def _gelu(x):
    hx = 0.5 * x
    return hx + hx * jnp.tanh(_gelu_inner(x))


def _dot(a, b):
    return jnp.dot(a, b, preferred_element_type=F32)


def _dot_nt(a, b):
    return lax.dot_general(a, b, (((1,), (1,)), ((), ())), preferred_element_type=F32)


def _ada_kernel(c_ref, w_ref, b_ref, o_ref):
    c = c_ref[...]
    cond = (c * jax.nn.sigmoid(c)).astype(BF16)
    o_ref[...] = _dot(cond, w_ref[...].astype(BF16)) + b_ref[...]


def _ada_call(c, ada_w, ada_b):
    depth, d, n = ada_w.shape
    b = c.shape[0]
    tn = n // 4
    return pl.pallas_call(
        _ada_kernel,
        grid=(depth, n // tn),
        in_specs=[
            pl.BlockSpec((b, d), lambda l, j: (0, 0)),
            pl.BlockSpec((None, d, tn), lambda l, j: (l, 0, j)),
            pl.BlockSpec((None, 1, tn), lambda l, j: (l, 0, j)),
        ],
        out_specs=pl.BlockSpec((None, b, tn), lambda l, j: (l, 0, j)),
        out_shape=jax.ShapeDtypeStruct((depth, b, n), F32),
        compiler_params=pltpu.CompilerParams(
            dimension_semantics=("arbitrary", "arbitrary"), vmem_limit_bytes=VMEM_LIMIT_BYTES),
        name="adaln",
    )(c, ada_w, ada_b.reshape(depth, 1, n))


def _pool_kernel(x_ref, mod_ref, g_ref, win_ref, wgrp_ref, scale_ref, wout_ref, o_ref, ubuf):
    tb, d = x_ref.shape
    grp = d // len(POOL_WINDOWS)
    s = pl.program_id(1)

    @pl.when(s == 0)
    def _():
        ubuf[0:POOL_HALO, :] = jnp.zeros((POOL_HALO, d), F32)

    x = x_ref[...]
    h = _rms_modulate(x, g_ref[...], mod_ref[0], mod_ref[1]).astype(BF16)
    u = _dot(h, win_ref[...])
    ubuf[POOL_HALO:POOL_HALO + tb, :] = u

    pos = s * tb + lax.broadcasted_iota(jnp.int32, (tb, 1), 0)
    t1 = (pos + 1).astype(F32)
    zs = []
    for gi, w in enumerate(POOL_WINDOWS):
        lo = gi * grp
        acc = u[:, lo:lo + grp]
        for k in range(1, w):
            acc = acc + ubuf[POOL_HALO - k:POOL_HALO - k + tb, lo:lo + grp]
        mean = acc / jnp.minimum(t1, float(w))
        p = (mean - u[:, lo:lo + grp]).astype(BF16)
        zs.append(_dot(p, wgrp_ref[gi]))
    z = (jnp.concatenate(zs, axis=-1) * scale_ref[...]).astype(BF16)
    y = _dot(z, wout_ref[...])
    o_ref[...] = x + mod_ref[2] * y
    ubuf[0:POOL_HALO, :] = ubuf[tb:tb + POOL_HALO, :]


def _pool_call(x, mod, g, w_in, w_grp, scale, w_out, *, tb):
    b, s, d = x.shape
    ng, grp = w_grp.shape[0], w_grp.shape[1]
    return pl.pallas_call(
        _pool_kernel,
        grid=(b, s // tb),
        in_specs=[
            pl.BlockSpec((None, tb, d), lambda i, j: (i, j, 0)),
            pl.BlockSpec((None, 6, 1, d), lambda i, j: (i, 0, 0, 0)),
            pl.BlockSpec((1, d), lambda i, j: (0, 0)),
            pl.BlockSpec((d, d), lambda i, j: (0, 0)),
            pl.BlockSpec((ng, grp, grp), lambda i, j: (0, 0, 0)),
            pl.BlockSpec((1, d), lambda i, j: (0, 0)),
            pl.BlockSpec((d, d), lambda i, j: (0, 0)),
        ],
        out_specs=pl.BlockSpec((None, tb, d), lambda i, j: (i, j, 0)),
        out_shape=jax.ShapeDtypeStruct((b, s, d), F32),
        scratch_shapes=[pltpu.VMEM((tb + POOL_HALO, d), F32)],
        compiler_params=pltpu.CompilerParams(
            dimension_semantics=("arbitrary", "arbitrary"), vmem_limit_bytes=VMEM_LIMIT_BYTES),
        name="pool_mixer",
    )(x, mod, g, w_in, w_grp, scale, w_out)


def _conv_kernel(x_ref, mod_ref, g_ref, win_ref, cw_ref, wout_ref, o_ref, zbuf):
    tb, d = x_ref.shape
    s = pl.program_id(1)

    @pl.when(s == 0)
    def _():
        zbuf[0:CONV_HALO, :] = jnp.zeros((CONV_HALO, d), F32)

    x = x_ref[...]
    h = _rms_modulate(x, g_ref[...], mod_ref[0], mod_ref[1]).astype(BF16)
    proj = _dot(h, win_ref[...])
    bg, cg, xt = proj[:, 0:d], proj[:, d:2 * d], proj[:, 2 * d:3 * d]
    z = cg * xt
    zbuf[CONV_HALO:CONV_HALO + tb, :] = z
    zc = z * cw_ref[CONV_WIDTH - 1:CONV_WIDTH, :]
    for k in range(1, CONV_WIDTH):
        zc = zc + zbuf[CONV_HALO - k:CONV_HALO - k + tb, :] * cw_ref[CONV_WIDTH - 1 - k:CONV_WIDTH - k, :]
    y = _dot((bg * zc).astype(BF16), wout_ref[...])
    o_ref[...] = x + mod_ref[2] * y
    zbuf[0:CONV_HALO, :] = zbuf[tb:tb + CONV_HALO, :]


def _conv_call(x, mod, g, w_in, conv_w, w_out, *, tb):
    b, s, d = x.shape
    return pl.pallas_call(
        _conv_kernel,
        grid=(b, s // tb),
        in_specs=[
            pl.BlockSpec((None, tb, d), lambda i, j: (i, j, 0)),
            pl.BlockSpec((None, 6, 1, d), lambda i, j: (i, 0, 0, 0)),
            pl.BlockSpec((1, d), lambda i, j: (0, 0)),
            pl.BlockSpec((d, 3 * d), lambda i, j: (0, 0)),
            pl.BlockSpec((CONV_WIDTH, d), lambda i, j: (0, 0)),
            pl.BlockSpec((d, d), lambda i, j: (0, 0)),
        ],
        out_specs=pl.BlockSpec((None, tb, d), lambda i, j: (i, j, 0)),
        out_shape=jax.ShapeDtypeStruct((b, s, d), F32),
        scratch_shapes=[pltpu.VMEM((tb + CONV_HALO, d), F32)],
        compiler_params=pltpu.CompilerParams(
            dimension_semantics=("arbitrary", "arbitrary"), vmem_limit_bytes=VMEM_LIMIT_BYTES),
        name="conv_mixer",
    )(x, mod, g, w_in, conv_w, w_out)


def _gmlp_kernel(x_ref, mod_ref, g_ref, win_ref, lng_ref, lnb_ref, ws_ref, bs_ref, wout_ref, o_ref):
    tb, d = x_ref.shape
    half = wout_ref.shape[0]
    gdim = half // GMLP_GROUPS

    x = x_ref[...]
    h = _rms_modulate(x, g_ref[...], mod_ref[0], mod_ref[1]).astype(BF16)
    z = _gelu(_dot(h, win_ref[...]))
    u, v = z[:, 0:half], z[:, half:2 * half]
    mu = jnp.mean(v, axis=-1, keepdims=True)
    vc = v - mu
    var = jnp.mean(vc * vc, axis=-1, keepdims=True)
    v = ((vc * lax.rsqrt(var + EPS)) * lng_ref[...] + lnb_ref[...]).astype(BF16)

    row = lax.broadcasted_iota(jnp.int32, (GMLP_BLOCK, GMLP_BLOCK), 0) // CHUNK
    col = lax.broadcasted_iota(jnp.int32, (GMLP_BLOCK, GMLP_BLOCK), 1) // CHUNK
    causal = col <= row
    blocks = []
    for n in range(tb // GMLP_BLOCK):
        r0 = n * GMLP_BLOCK
        cols = []
        for gi in range(GMLP_GROUPS):
            ws = jnp.where(causal, ws_ref[gi], 0.0).astype(BF16)
            vs = _dot(ws, v[r0:r0 + GMLP_BLOCK, gi * gdim:(gi + 1) * gdim]) + bs_ref[gi]
            cols.append(vs)
        blocks.append(jnp.concatenate(cols, axis=-1))
    vs = jnp.concatenate(blocks, axis=0)
    y = _dot((u * vs).astype(BF16), wout_ref[...])
    o_ref[...] = x + mod_ref[2] * y


def _gmlp_call(x, mod, g, w_in, ln_g, ln_b, w_s, b_s, w_out, *, tb):
    b, s, d = x.shape
    half = w_out.shape[0]
    nblk = s // tb
    const = dict(pipeline_mode=pl.Buffered(1))
    return pl.pallas_call(
        _gmlp_kernel,
        grid=(b, nblk),
        in_specs=[
            pl.BlockSpec((None, tb, d), lambda i, j: (i, j, 0)),
            pl.BlockSpec((None, 6, 1, d), lambda i, j: (i, 0, 0, 0)),
            pl.BlockSpec((1, d), lambda i, j: (0, 0)),
            pl.BlockSpec((d, 2 * half), lambda i, j: (0, 0), **const),
            pl.BlockSpec((1, half), lambda i, j: (0, 0)),
            pl.BlockSpec((1, half), lambda i, j: (0, 0)),
            pl.BlockSpec((GMLP_GROUPS, GMLP_BLOCK, GMLP_BLOCK), lambda i, j: (0, 0, 0)),
            pl.BlockSpec((GMLP_GROUPS, GMLP_BLOCK, 1), lambda i, j: (0, 0, 0)),
            pl.BlockSpec((half, d), lambda i, j: (0, 0), **const),
        ],
        out_specs=pl.BlockSpec((None, tb, d), lambda i, j: (i, j, 0)),
        out_shape=jax.ShapeDtypeStruct((b, s, d), F32),
        compiler_params=pltpu.CompilerParams(
            dimension_semantics=("arbitrary", "arbitrary"), vmem_limit_bytes=VMEM_LIMIT_BYTES),
        name="gmlp_mixer",
    )(x, mod, g, w_in, ln_g, ln_b, w_s, b_s, w_out)


def _merge_sort_network(lo, hi):
    def merge(lo, hi, r):
        step = r * 2
        if step < hi - lo:
            yield from merge(lo, hi, step)
            yield from merge(lo + r, hi, step)
            yield from [(i, i + r) for i in range(lo + r, hi - r, step)]
        else:
            yield (lo, lo + r)

    if hi - lo >= 1:
        mid = lo + (hi - lo) // 2
        yield from _merge_sort_network(lo, mid)
        yield from _merge_sort_network(mid + 1, hi)
        yield from merge(lo, hi, 1)


def _sorted_top(s, k):
    n = s.shape[0] // SUBLANES
    assert n == k and n & (n - 1) == 0
    v = [s[g * SUBLANES:(g + 1) * SUBLANES] for g in range(n)]
    for i, j in _merge_sort_network(0, n - 1):
        v[i], v[j] = jnp.maximum(v[i], v[j]), jnp.minimum(v[i], v[j])
    rows = lax.broadcasted_iota(jnp.int32, (k, s.shape[1]), 0)
    out = jnp.zeros((k, s.shape[1]), F32)
    for it in range(k):
        m = jnp.max(v[0], axis=0, keepdims=True)
        out = jnp.where(rows == it, m, out)
        eq = v[0] == m
        depth = k - it
        for dd in range(depth - 1):
            v[dd] = jnp.where(eq, v[dd + 1], v[dd])
        v[depth - 1] = jnp.where(eq, -jnp.inf, v[depth - 1])
    return out


def _route_tile(s0, s1):
    k = PEER_TOPK
    a = _sorted_top(s0, k)
    b = _sorted_top(s1, k)
    rank1 = jnp.full(s1.shape, float(k), F32)
    for r in range(k - 1, -1, -1):
        rank1 = jnp.where(s1 == b[r:r + 1], float(r), rank1)
    parts = [a[0:1] + b]
    for r in range(1, 8):
        parts.append(a[r:r + 1] + b[0:8])
    parts.append(a[8:16] + b[0:1])
    cand = jnp.concatenate(parts, axis=0)
    top = cand[0:1]
    work = cand
    tau = top
    for _ in range(k):
        tau = jnp.max(work, axis=0, keepdims=True)
        work = jnp.where(work == tau, -jnp.inf, work)
    z = jnp.sum(jnp.where(cand >= tau, jnp.exp(cand - top), 0.0), axis=0, keepdims=True)

    cnt_r = jnp.zeros(a.shape, F32)
    for c in range(k):
        cnt_r = cnt_r + jnp.where(a + b[c:c + 1] >= tau, 1.0, 0.0)
    cnt = jnp.zeros(s0.shape, F32)
    for r in range(k):
        cnt = jnp.where(s0 == a[r:r + 1], cnt_r[r:r + 1], cnt)

    a0 = 0.5 * jnp.exp(s0 - a[0:1]) / z
    e1 = jnp.exp(s1 - b[0:1])
    return rank1, e1, cnt, a0


def _peer_kernel(x_ref, mod_ref, g_ref, fg_ref, wq_ref, keys_ref, u_ref, vt_ref, o_ref,
                 ht_scr, q_scr, col_scr, cnt_scr, a0_scr, y0_scr, y1_scr, p0_scr, p1_scr,
                 acc_scr,
                 *, route_lanes, mxu_rows, final_norm):
    tb, d = x_ref.shape
    ec = u_ref.shape[0]
    heads, _, nk, dh = keys_ref.shape
    c = pl.program_id(1)
    nc = pl.num_programs(1) - 2
    rows_per_step = ec // nk
    n_lt = tb // LANES
    n_rt = tb // route_lanes
    n_jg = nk // SUBLANES

    y_bufs = (y0_scr, y1_scr)
    p_bufs = (p0_scr, p1_scr)

    def stage_a(slot):
        part = {}

        def item(m, k):
            rows = slice(m * mxu_rows, (m + 1) * mxu_rows)
            ks = slice(k * mxu_rows, (k + 1) * mxu_rows)
            prod = _dot(u_ref[rows, ks], ht_scr[ks, :])
            part[m] = prod if k == 0 else part[m] + prod
            if k == d // mxu_rows - 1:
                y_bufs[slot][rows, :] = part.pop(m)

        return [functools.partial(item, m, k) for m in range(ec // mxu_rows)
                for k in range(d // mxu_rows)]

    def stage_b(slot):
        cb = c - 1
        y_scr, p_scr = y_bufs[slot], p_bufs[slot]

        def item(ii, lt):
            grp = cb * (rows_per_step // SUBLANES) + ii // SUBLANES
            sl = ii % SUBLANES
            lanes = slice(lt * LANES, (lt + 1) * LANES)
            cnt_rows, a0_rows = [], []
            for hd in range(heads):
                cnt_rows.append(jnp.broadcast_to(cnt_scr[hd, grp, sl:sl + 1, lanes],
                                                 (SUBLANES, LANES)))
                a0_rows.append(jnp.broadcast_to(a0_scr[hd, grp, sl:sl + 1, lanes],
                                                (SUBLANES, LANES)))
            for jp in range(n_jg // 2):
                gated = []
                for jg in (2 * jp, 2 * jp + 1):
                    w = None
                    base = (lt * n_jg + jg) * 2 * SUBLANES
                    for hd in range(heads):
                        rank = col_scr[hd, base:base + SUBLANES, :]
                        e1 = col_scr[hd, base + SUBLANES:base + 2 * SUBLANES, :]
                        p = a0_rows[hd] * e1
                        if w is None:
                            w = jnp.where(rank < cnt_rows[hd], p, 0.0)
                        else:
                            w = jnp.where(rank < cnt_rows[hd], w + p, w)
                    r0 = ii * nk + jg * SUBLANES
                    y = y_scr[r0:r0 + SUBLANES, lanes]
                    wy = w * y
                    gated.append(wy + wy * jnp.tanh(_gelu_inner(y)))
                r0 = ii * nk + jp * PACKED_ROWS
                p_scr[r0:r0 + PACKED_ROWS, lanes] = jnp.concatenate(gated, axis=0).astype(BF16)

        return [functools.partial(item, ii, lt) for ii in range(rows_per_step)
                for lt in range(n_lt)]

    def stage_c(slot):
        part = {}

        def item(m, k):
            rows = slice(m * mxu_rows, (m + 1) * mxu_rows)
            ks = slice(k * mxu_rows, (k + 1) * mxu_rows)
            prod = lax.dot_general(vt_ref[ks, rows], p_bufs[slot][ks, :],
                                   (((0,), (0,)), ((), ())), preferred_element_type=F32)
            part[m] = prod if k == 0 else part[m] + prod
            if k == ec // mxu_rows - 1:
                acc_scr[rows, :] += part.pop(m)

        return [functools.partial(item, m, k) for m in range(d // mxu_rows)
                for k in range(ec // mxu_rows)]

    def run(mxu_items, valu_items):
        n_m, n_v = len(mxu_items), len(valu_items)
        if n_m == 0 or n_v == 0:
            for f in mxu_items + valu_items:
                f()
            return
        done_v = 0
        for k, f in enumerate(mxu_items):
            f()
            upto = ((k + 1) * n_v) // n_m
            for g in valu_items[done_v:upto]:
                g()
            done_v = upto

    def interleave(a, b):
        out = []
        for k in range(max(len(a), len(b))):
            out += a[k:k + 1] + b[k:k + 1]
        return out

    @pl.when(c == 0)
    def _first():
        x = x_ref[...]
        h = _rms_modulate(x, g_ref[...], mod_ref[3], mod_ref[4])
        ht_scr[...] = h.T.astype(BF16)
        q_scr[...] = _dot(wq_ref[...], ht_scr[...]).astype(BF16)
        acc_scr[...] = jnp.zeros_like(acc_scr)

        def route_head(hd, lane_group):
            lanes = pl.ds(pl.multiple_of(lane_group * route_lanes, route_lanes), route_lanes)
            q0 = q_scr[pl.ds(pl.multiple_of(hd * 2 * dh, dh), dh), lanes]
            q1 = q_scr[pl.ds(pl.multiple_of(hd * 2 * dh + dh, dh), dh), lanes]
            s0 = _dot(keys_ref[hd, 0], q0)
            s1 = _dot(keys_ref[hd, 1], q1)
            rank1, e1, cnt, a0 = _route_tile(s0, s1)
            for ls in range(route_lanes // LANES):
                tiles = []
                for jg in range(n_jg):
                    blk = (slice(jg * SUBLANES, (jg + 1) * SUBLANES),
                           slice(ls * LANES, (ls + 1) * LANES))
                    tiles += [rank1[blk], e1[blk]]
                lt = lane_group * (route_lanes // LANES) + ls
                start = pl.multiple_of(lt * (n_jg * 2 * SUBLANES), n_jg * 2 * SUBLANES)
                col_scr[hd, pl.ds(start, n_jg * 2 * SUBLANES), :] = jnp.concatenate(tiles, axis=0)
            cnt_scr[hd, :, :, lanes] = cnt.reshape(nk // SUBLANES, SUBLANES, route_lanes)
            a0_scr[hd, :, :, lanes] = a0.reshape(nk // SUBLANES, SUBLANES, route_lanes)

        def route(it, carry):
            for k in range(ROUTE_HEADS):
                route_head((it // n_rt) * ROUTE_HEADS + k, it % n_rt)
            return carry

        lax.fori_loop(0, (heads // ROUTE_HEADS) * n_rt, route, 0)
        run(stage_a(0), [])

    @pl.when(c == 1)
    def _fill():
        run(stage_a(1), stage_b(0))

    steady = jnp.logical_and(c >= 2, c < nc)

    @pl.when(jnp.logical_and(steady, c % 2 == 0))
    def _steady_even():
        run(interleave(stage_a(0), stage_c(0)), stage_b(1))

    @pl.when(jnp.logical_and(steady, c % 2 == 1))
    def _steady_odd():
        run(interleave(stage_a(1), stage_c(1)), stage_b(0))

    @pl.when(c == nc)
    def _drain():
        run(stage_c(0), stage_b(1))

    @pl.when(c == nc + 1)
    def _last():
        run(stage_c(1), [])
        x = x_ref[...]
        out = x + mod_ref[5] * acc_scr[...].T
        if final_norm:
            r = lax.rsqrt(jnp.mean(out * out, axis=-1, keepdims=True) + EPS)
            out = out * r * fg_ref[...]
        o_ref[...] = out


def _peer_call(x2, mod, g, final_g, wq_t, keys, u_tab, vt_tab, *, layer, seq, tb, ec, route_lanes, mxu_rows,
               final_norm):
    t, d = x2.shape
    n_exp = u_tab.shape[1]
    heads, _, nk, dh = keys.shape[1:]
    tiles_per_seq = seq // tb
    nc = n_exp // ec
    assert ec % (SUBLANES * nk) == 0 and nk % PACKED_ROWS == 0 and nc >= 4 and nc % 2 == 0
    assert seq % tb == 0 and tb % route_lanes == 0 and route_lanes % LANES == 0
    kern = functools.partial(_peer_kernel, route_lanes=route_lanes, mxu_rows=mxu_rows,
                             final_norm=final_norm)
    return pl.pallas_call(
        kern,
        grid=(t // tb, nc + 2),
        in_specs=[
            pl.BlockSpec((tb, d), lambda i, c: (i, 0)),
            pl.BlockSpec((None, 6, 1, d), lambda i, c: (i // tiles_per_seq, 0, 0, 0)),
            pl.BlockSpec((1, d), lambda i, c: (0, 0)),
            pl.BlockSpec((1, d), lambda i, c: (0, 0)),
            pl.BlockSpec((None, heads * 2 * dh, d), lambda i, c: (layer, 0, 0)),
            pl.BlockSpec((None, heads, 2, nk, dh), lambda i, c: (layer, 0, 0, 0, 0)),
            pl.BlockSpec((None, ec, d), lambda i, c: (layer, jnp.minimum(c, nc - 1), 0)),
            pl.BlockSpec((None, ec, d), lambda i, c: (layer, jnp.clip(c - 2, 0, nc - 1), 0)),
        ],
        out_specs=pl.BlockSpec((tb, d), lambda i, c: (i, 0)),
        out_shape=jax.ShapeDtypeStruct((t, d), F32),
        scratch_shapes=[
            pltpu.VMEM((d, tb), BF16),
            pltpu.VMEM((heads * 2 * dh, tb), BF16),
            pltpu.VMEM((heads, (tb // LANES) * (nk // SUBLANES) * 2 * SUBLANES + SUBLANES, LANES),
                       F32),
            pltpu.VMEM((heads, nk // SUBLANES, SUBLANES, tb), F32),
            pltpu.VMEM((heads, nk // SUBLANES, SUBLANES, tb), F32),
            pltpu.VMEM((ec, tb), F32),
            pltpu.VMEM((ec, tb), F32),
            pltpu.VMEM((ec, tb), BF16),
            pltpu.VMEM((ec, tb), BF16),
            pltpu.VMEM((d, tb), F32),
        ],
        compiler_params=pltpu.CompilerParams(
            dimension_semantics=("arbitrary", "arbitrary"), vmem_limit_bytes=VMEM_LIMIT_BYTES),
        name="peer",
    )(x2, mod, g, final_g, wq_t, keys, u_tab, vt_tab)


def _tiles(seq, n_exp):
    tb_mix = min(seq, 512)
    tb_gmlp = min(seq, 256)
    tb_peer = min(seq, 512)
    ec = min(n_exp // 4, 1024)
    route_lanes = min(tb_peer, 256)
    mxu_rows = 256
    return tb_mix, tb_gmlp, tb_peer, ec, route_lanes, mxu_rows


def kernel(x, c, ada_w, ada_b, norm_g, pool_w_in, pool_w_grp, pool_scale, pool_w_out, conv_w_in, conv_w, conv_w_out, gmlp_w_in, gmlp_ln_g, gmlp_ln_b, gmlp_w_s, gmlp_b_s, gmlp_w_out, peer_w_q, peer_keys, peer_u, peer_v, final_g):
    b, seq, d = x.shape
    depth = ada_w.shape[0]
    n_exp = peer_u.shape[1]
    tb_mix, tb_gmlp, tb_peer, ec, route_lanes, mxu_rows = _tiles(seq, n_exp)

    mod = _ada_call(c, ada_w, ada_b).reshape(depth, b, 6, 1, d)

    wq_t = jnp.swapaxes(peer_w_q, 1, 2).astype(BF16)
    keys = peer_keys.astype(BF16)
    u_tab = peer_u.astype(BF16)
    vt_tab = peer_v.astype(BF16)
    fg = final_g.reshape(1, d)

    for i in range(depth):
        kind, j = i % N_MIXERS, i // N_MIXERS
        g1 = norm_g[i, 0].reshape(1, d)
        if kind == 0:
            x = _pool_call(x, mod[i], g1, pool_w_in[j].astype(BF16), pool_w_grp[j].astype(BF16),
                           pool_scale[j].reshape(1, d), pool_w_out[j].astype(BF16), tb=tb_mix)
        elif kind == 1:
            x = _conv_call(x, mod[i], g1, conv_w_in[j].astype(BF16), conv_w[j],
                           conv_w_out[j].astype(BF16), tb=tb_mix)
        else:
            half = gmlp_w_out.shape[1]
            x = _gmlp_call(x, mod[i], g1, gmlp_w_in[j].astype(BF16), gmlp_ln_g[j].reshape(1, half),
                           gmlp_ln_b[j].reshape(1, half), gmlp_w_s[j],
                           gmlp_b_s[j].reshape(GMLP_GROUPS, GMLP_BLOCK, 1),
                           gmlp_w_out[j].astype(BF16), tb=tb_gmlp)
        x = _peer_call(x.reshape(b * seq, d), mod[i], norm_g[i, 1].reshape(1, d), fg, wq_t, keys,
                       u_tab, vt_tab, layer=i, seq=seq, tb=tb_peer, ec=ec, route_lanes=route_lanes, mxu_rows=mxu_rows,
                       final_norm=(i == depth - 1)).reshape(b, seq, d)
    return x
```
